```python
import math
import jax, jax.numpy as jnp
from jax import lax
import numpy as np

D_MODEL = 1024
BATCH = 4
SEQ = 4096
DEPTH = 4
DEC_BATCH = 32
DEC_SEQ = 1
PAST_LEN = 8192
PAGE_SIZE = 128

N_HEADS = 8
HEAD_DIM = 128
N_KV_HEADS = 4
KV_GROUP = N_HEADS // N_KV_HEADS
ATTN_WIDTH = N_HEADS * HEAD_DIM
KV_WIDTH = N_KV_HEADS * HEAD_DIM
N_IDX_HEADS = 8
IDX_DIM = 64
TOPK_MAX = 256
Q_BLOCK = 128
SSM_WIDTH = 512
SSM_GROUP = 16
N_SSM_GROUPS = SSM_WIDTH // SSM_GROUP
SSM_STATE = 64
D_FF = 2816
N_EXPERTS = 8
TOP_K_EXPERTS = 2
D_FF_EXPERT = 1408
N_DENSE = (DEPTH + 1) // 2
N_MOE = DEPTH // 2
ALPHA = (2.0 * DEPTH) ** 0.25
BETA = (8.0 * DEPTH) ** -0.25
LN_EPS = 1e-5
NEG_BIG = -1e30
IN_SIZES = (ATTN_WIDTH, KV_WIDTH, KV_WIDTH, N_IDX_HEADS * IDX_DIM, IDX_DIM, N_IDX_HEADS, SSM_WIDTH, D_MODEL, D_MODEL)
IN_WIDTH = ATTN_WIDTH + 2 * KV_WIDTH + N_IDX_HEADS * IDX_DIM + IDX_DIM + N_IDX_HEADS + SSM_WIDTH + 2 * D_MODEL

kernel_name = 'dsa_s5_gated_hybrid_decoder_step'


def layer_norm(x, g, b):
    xf = x.astype(jnp.float32)
    mu = jnp.mean(xf, axis=-1, keepdims=True)
    xc = xf - mu
    var = jnp.mean(xc * xc, axis=-1, keepdims=True)
    return (xc * lax.rsqrt(var + LN_EPS) * g.astype(jnp.float32) + b.astype(jnp.float32)).astype(x.dtype)


def split_in(h):
    idx = np.cumsum(np.array(IN_SIZES))[:-1].tolist()
    return jnp.split(h, idx, axis=-1)


def take_rows(arr, idx):
    return jax.vmap(lambda a, i: a[i])(arr, idx)


def indexer_topk(qi, wi, ki, qpos, topk):
    dots = jnp.einsum('bthd,bsd->bths', qi.astype(jnp.float32), ki.astype(jnp.float32))
    score = jnp.einsum('bths,bth->bts', jax.nn.relu(dots), wi.astype(jnp.float32))
    causal = jnp.arange(ki.shape[1])[None, :] <= qpos[:, None]
    score = jnp.where(causal[None], score, -jnp.inf)
    _, idx = lax.top_k(score, topk)
    valid = idx <= qpos[None, :, None]
    return idx, valid


def sparse_attend(q, k_sel, v_sel, valid):
    B, T = q.shape[:2]
    qg = q.reshape(B, T, N_KV_HEADS, KV_GROUP, HEAD_DIM).astype(jnp.float32)
    logits = jnp.einsum('btkgd,btjkd->btkgj', qg, k_sel.astype(jnp.float32)) * (HEAD_DIM ** -0.5)
    logits = jnp.where(valid[:, :, None, None, :], logits, NEG_BIG)
    p = jax.nn.softmax(logits, axis=-1)
    o = jnp.einsum('btkgj,btjkd->btkgd', p, v_sel.astype(jnp.float32))
    return o.reshape(B, T, ATTN_WIDTH).astype(q.dtype)


def prompt_attention(q, k, v, qi, ki, wi):
    B, S = q.shape[:2]
    topk = min(TOPK_MAX, S // 4)
    nb = S // Q_BLOCK

    def to_blocks(t):
        return jnp.moveaxis(t.reshape((B, nb, Q_BLOCK) + t.shape[2:]), 1, 0)

    starts = jnp.arange(nb, dtype=jnp.int32) * Q_BLOCK

    def block(args):
        qb, qib, wib, start = args
        qpos = start + jnp.arange(Q_BLOCK, dtype=jnp.int32)
        idx, valid = indexer_topk(qib, wib, ki, qpos, topk)
        return sparse_attend(qb, take_rows(k, idx), take_rows(v, idx), valid)

    out = lax.map(block, (to_blocks(q), to_blocks(qi), to_blocks(wi), starts))
    return jnp.moveaxis(out, 0, 1).reshape(B, S, ATTN_WIDTH)


def sample_attention(q, k_new, v_new, qi, ki_new, wi, cache_k, cache_v, cache_kidx, page_table, layer):
    DB, T = q.shape[:2]
    past = page_table.shape[1] * PAGE_SIZE
    topk = min(TOPK_MAX, (past + T) // 4)
    ki_past = cache_kidx[layer, page_table].reshape(DB, past, IDX_DIM)
    ki_all = jnp.concatenate([ki_past, ki_new.astype(ki_past.dtype)], axis=1)
    qpos = past + jnp.arange(T, dtype=jnp.int32)
    idx, valid = indexer_topk(qi, wi, ki_all, qpos, topk)
    in_past = idx < past
    pidx = jnp.minimum(idx, past - 1)
    phys = jax.vmap(lambda pt, p: pt[p])(page_table, pidx // PAGE_SIZE)
    off = pidx % PAGE_SIZE
    nidx = jnp.clip(idx - past, 0, T - 1)

    def gather(cache, new):
        past_rows = cache[layer, phys, off]
        new_rows = take_rows(new, nidx).astype(past_rows.dtype)
        return jnp.where(in_past[..., None, None], past_rows, new_rows)

    return sparse_attend(q, gather(cache_k, k_new), gather(cache_v, v_new), valid)


def s5_branch(u, h0_re, h0_im, a_re, a_im, log_dt, b_re, b_im, c_re, c_im, d_skip, w_glu, b_glu):
    f32 = jnp.float32
    B, T = u.shape[:2]
    uf = u.astype(f32)
    ug = uf.reshape(B, T, N_SSM_GROUPS, SSM_GROUP)
    lr, li = a_re.astype(f32), a_im.astype(f32)
    dt = jnp.exp(log_dt.astype(f32))[:, None]
    mag = jnp.exp(lr * dt)
    ab_re, ab_im = mag * jnp.cos(li * dt), mag * jnp.sin(li * dt)
    den = lr * lr + li * li
    nr = ab_re - 1.0
    f_re = (nr * lr + ab_im * li) / den
    f_im = (ab_im * lr - nr * li) / den
    br, bi = b_re.astype(f32), b_im.astype(f32)
    bb_re = f_re[..., None] * br - f_im[..., None] * bi
    bb_im = f_re[..., None] * bi + f_im[..., None] * br
    bu_re = jnp.einsum('gpc,btgc->btgp', bb_re, ug)
    bu_im = jnp.einsum('gpc,btgc->btgp', bb_im, ug)
    e_re = jnp.concatenate([h0_re.astype(f32)[:, None], bu_re], axis=1)
    e_im = jnp.concatenate([h0_im.astype(f32)[:, None], bu_im], axis=1)
    a_re_b = jnp.broadcast_to(ab_re, e_re.shape)
    a_im_b = jnp.broadcast_to(ab_im, e_re.shape)

    def combine(e1, e2):
        a1r, a1i, b1r, b1i = e1
        a2r, a2i, b2r, b2i = e2
        return (a2r * a1r - a2i * a1i, a2r * a1i + a2i * a1r,
                a2r * b1r - a2i * b1i + b2r, a2r * b1i + a2i * b1r + b2i)

    _, _, h_re, h_im = lax.associative_scan(combine, (a_re_b, a_im_b, e_re, e_im), axis=1)
    h_re, h_im = h_re[:, 1:], h_im[:, 1:]
    y = jnp.einsum('gcp,btgp->btgc', c_re.astype(f32), h_re) - jnp.einsum('gcp,btgp->btgc', c_im.astype(f32), h_im)
    y = y.reshape(B, T, SSM_WIDTH) + d_skip.astype(f32) * uf
    y = jax.nn.gelu(y)
    y = y * jax.nn.sigmoid(y @ w_glu.astype(f32) + b_glu.astype(f32))
    return y.astype(u.dtype), h_re[:, -1], h_im[:, -1]


def token_mixer(x, attend, h0_re, h0_im, w_in, w_attn_proj, w_ssm_proj, w_out, ssm_params):
    B, T = x.shape[:2]
    q, k, v, qi, ki, wi, u, ga, gb = split_in(x @ w_in)
    q = q.reshape(B, T, N_HEADS, HEAD_DIM)
    k = k.reshape(B, T, N_KV_HEADS, HEAD_DIM)
    v = v.reshape(B, T, N_KV_HEADS, HEAD_DIM)
    qi = qi.reshape(B, T, N_IDX_HEADS, IDX_DIM)
    attn = attend(q, k, v, qi, ki, wi)
    ssm, hr, hi = s5_branch(u, h0_re, h0_im, *ssm_params)
    merged = jax.nn.sigmoid(ga) * (attn @ w_attn_proj) + jax.nn.sigmoid(gb) * (ssm @ w_ssm_proj)
    return merged @ w_out, k, v, ki, hr, hi


def swiglu(x, w1, w3, w2):
    return (jax.nn.silu(x @ w1) * (x @ w3)) @ w2


def moe_swiglu(x, w_router, b_router, w1, w3, w2):
    shp = x.shape
    xt = x.reshape(-1, shp[-1])
    logits = (xt @ w_router).astype(jnp.float32) + b_router.astype(jnp.float32)
    top_v, top_i = lax.top_k(logits, TOP_K_EXPERTS)
    gates = jax.nn.softmax(top_v, axis=-1)
    comb = jnp.sum(jax.nn.one_hot(top_i, N_EXPERTS, dtype=jnp.float32) * gates[..., None], axis=1)
    out = jnp.zeros(xt.shape, jnp.float32)
    for e in range(N_EXPERTS):
        out = out + comb[:, e:e + 1] * swiglu(xt, w1[e], w3[e], w2[e]).astype(jnp.float32)
    return out.astype(x.dtype).reshape(shp)


def setup_inputs(seed: int = 0) -> dict:
    key = jax.random.key(seed)
    ks = iter(jax.random.split(key, 48))
    f32 = jnp.float32

    def nrm(shape, scale):
        return jax.random.normal(next(ks), shape, f32) * scale

    n_pages = PAST_LEN // PAGE_SIZE
    n_used = DEC_BATCH * n_pages
    n_pool = n_used + (n_used + 3) // 4
    G, P, C = N_SSM_GROUPS, SSM_STATE, SSM_GROUP

    x_prompt = nrm((BATCH, SEQ, D_MODEL), 1.0)
    x_sample = nrm((DEC_BATCH, DEC_SEQ, D_MODEL), 1.0)
    cache_k = nrm((DEPTH, n_pool, PAGE_SIZE, N_KV_HEADS, HEAD_DIM), 1.0)
    cache_v = nrm((DEPTH, n_pool, PAGE_SIZE, N_KV_HEADS, HEAD_DIM), 1.0)
    cache_kidx = nrm((DEPTH, n_pool, PAGE_SIZE, IDX_DIM), 1.0)
    state_ssm_re = nrm((DEPTH, DEC_BATCH, G, P), 0.3)
    state_ssm_im = nrm((DEPTH, DEC_BATCH, G, P), 0.3)
    page_table = jax.random.permutation(next(ks), n_pool)[:n_used].reshape(DEC_BATCH, n_pages).astype(jnp.int32)

    ln1_g = 1.0 + nrm((DEPTH, D_MODEL), 0.01)
    ln1_b = nrm((DEPTH, D_MODEL), 0.01)
    w_in = nrm((DEPTH, D_MODEL, IN_WIDTH), D_MODEL ** -0.5)
    w_attn_proj = nrm((DEPTH, ATTN_WIDTH, D_MODEL), ATTN_WIDTH ** -0.5)
    w_ssm_proj = nrm((DEPTH, SSM_WIDTH, D_MODEL), SSM_WIDTH ** -0.5)
    w_out = nrm((DEPTH, D_MODEL, D_MODEL), BETA * D_MODEL ** -0.5)
    ssm_a_re = -0.5 + nrm((DEPTH, G, P), 0.01)
    ssm_a_im = jnp.pi * jnp.arange(P, dtype=f32)[None, None, :] + nrm((DEPTH, G, P), 0.01)
    ssm_log_dt = jax.random.uniform(next(ks), (DEPTH, G), f32, math.log(0.001), math.log(0.1))
    ssm_b_re = nrm((DEPTH, G, P, C), (2.0 * C) ** -0.5)
    ssm_b_im = nrm((DEPTH, G, P, C), (2.0 * C) ** -0.5)
    ssm_c_re = nrm((DEPTH, G, C, P), (2.0 * P) ** -0.5)
    ssm_c_im = nrm((DEPTH, G, C, P), (2.0 * P) ** -0.5)
    ssm_d = nrm((DEPTH, SSM_WIDTH), 1.0)
    ssm_w_glu = nrm((DEPTH, SSM_WIDTH, SSM_WIDTH), SSM_WIDTH ** -0.5)
    ssm_b_glu = nrm((DEPTH, SSM_WIDTH), 0.01)
    ln2_g = 1.0 + nrm((DEPTH, D_MODEL), 0.01)
    ln2_b = nrm((DEPTH, D_MODEL), 0.01)
    ffn_w1 = nrm((N_DENSE, D_MODEL, D_FF), D_MODEL ** -0.5)
    ffn_w3 = nrm((N_DENSE, D_MODEL, D_FF), D_MODEL ** -0.5)
    ffn_w2 = nrm((N_DENSE, D_FF, D_MODEL), BETA * D_FF ** -0.5)
    moe_w_router = nrm((N_MOE, D_MODEL, N_EXPERTS), D_MODEL ** -0.5)
    moe_b_router = nrm((N_MOE, N_EXPERTS), 0.01)
    moe_w1 = nrm((N_MOE, N_EXPERTS, D_MODEL, D_FF_EXPERT), D_MODEL ** -0.5)
    moe_w3 = nrm((N_MOE, N_EXPERTS, D_MODEL, D_FF_EXPERT), D_MODEL ** -0.5)
    moe_w2 = nrm((N_MOE, N_EXPERTS, D_FF_EXPERT, D_MODEL), BETA * D_FF_EXPERT ** -0.5)
    return {'x_prompt': x_prompt, 'x_sample': x_sample, 'cache_k': cache_k, 'cache_v': cache_v,
            'cache_kidx': cache_kidx, 'state_ssm_re': state_ssm_re, 'state_ssm_im': state_ssm_im,
            'page_table': page_table, 'ln1_g': ln1_g, 'ln1_b': ln1_b, 'w_in': w_in,
            'w_attn_proj': w_attn_proj, 'w_ssm_proj': w_ssm_proj, 'w_out': w_out,
            'ssm_a_re': ssm_a_re, 'ssm_a_im': ssm_a_im, 'ssm_log_dt': ssm_log_dt,
            'ssm_b_re': ssm_b_re, 'ssm_b_im': ssm_b_im, 'ssm_c_re': ssm_c_re, 'ssm_c_im': ssm_c_im,
            'ssm_d': ssm_d, 'ssm_w_glu': ssm_w_glu, 'ssm_b_glu': ssm_b_glu,
            'ln2_g': ln2_g, 'ln2_b': ln2_b, 'ffn_w1': ffn_w1, 'ffn_w3': ffn_w3, 'ffn_w2': ffn_w2,
            'moe_w_router': moe_w_router, 'moe_b_router': moe_b_router,
            'moe_w1': moe_w1, 'moe_w3': moe_w3, 'moe_w2': moe_w2}


def reference(x_prompt, x_sample, cache_k, cache_v, cache_kidx, state_ssm_re, state_ssm_im, page_table,
              ln1_g, ln1_b, w_in, w_attn_proj, w_ssm_proj, w_out,
              ssm_a_re, ssm_a_im, ssm_log_dt, ssm_b_re, ssm_b_im, ssm_c_re, ssm_c_im,
              ssm_d, ssm_w_glu, ssm_b_glu, ln2_g, ln2_b, ffn_w1, ffn_w3, ffn_w2,
              moe_w_router, moe_b_router, moe_w1, moe_w3, moe_w2):
    xp, xs = x_prompt, x_sample
    kp, vp, kip, hrp, hip = [], [], [], [], []
    kss, vss, kis, hrs, his = [], [], [], [], []
    for l in range(DEPTH):
        ssm_p = (ssm_a_re[l], ssm_a_im[l], ssm_log_dt[l], ssm_b_re[l], ssm_b_im[l],
                 ssm_c_re[l], ssm_c_im[l], ssm_d[l], ssm_w_glu[l], ssm_b_glu[l])
        zeros = jnp.zeros((xp.shape[0], N_SSM_GROUPS, SSM_STATE), jnp.float32)
        mix_p, k_p, v_p, ki_p, hr_p, hi_p = token_mixer(
            xp, prompt_attention, zeros, zeros, w_in[l], w_attn_proj[l], w_ssm_proj[l], w_out[l], ssm_p)
        xp = layer_norm(ALPHA * xp + mix_p, ln1_g[l], ln1_b[l])
        attend_s = (lambda q, k, v, qi, ki, wi, _l=l: sample_attention(
            q, k, v, qi, ki, wi, cache_k, cache_v, cache_kidx, page_table, _l))
        mix_s, k_s, v_s, ki_s, hr_s, hi_s = token_mixer(
            xs, attend_s, state_ssm_re[l], state_ssm_im[l], w_in[l], w_attn_proj[l], w_ssm_proj[l], w_out[l], ssm_p)
        xs = layer_norm(ALPHA * xs + mix_s, ln1_g[l], ln1_b[l])
        j = l // 2
        if l % 2 == 0:
            fp = swiglu(xp, ffn_w1[j], ffn_w3[j], ffn_w2[j])
            fs = swiglu(xs, ffn_w1[j], ffn_w3[j], ffn_w2[j])
        else:
            fp = moe_swiglu(xp, moe_w_router[j], moe_b_router[j], moe_w1[j], moe_w3[j], moe_w2[j])
            fs = moe_swiglu(xs, moe_w_router[j], moe_b_router[j], moe_w1[j], moe_w3[j], moe_w2[j])
        xp = layer_norm(ALPHA * xp + fp, ln2_g[l], ln2_b[l])
        xs = layer_norm(ALPHA * xs + fs, ln2_g[l], ln2_b[l])
        kp.append(k_p); vp.append(v_p); kip.append(ki_p); hrp.append(hr_p); hip.append(hi_p)
        kss.append(k_s); vss.append(v_s); kis.append(ki_s); hrs.append(hr_s); his.append(hi_s)
    return (xp, xs,
            jnp.stack(kp), jnp.stack(vp), jnp.stack(kip), jnp.stack(hrp), jnp.stack(hip),
            jnp.stack(kss), jnp.stack(vss), jnp.stack(kis), jnp.stack(hrs), jnp.stack(his))
```

```python
import functools
import math

import jax
import jax.numpy as jnp
import numpy as np
from jax import lax
from jax.experimental import pallas as pl
from jax.experimental.pallas import tpu as pltpu

F32 = jnp.float32
BF16 = jnp.bfloat16
I32 = jnp.int32

N_HEADS = 8
HEAD_DIM = 128
N_KV_HEADS = 4
KV_GROUP = N_HEADS // N_KV_HEADS
ATTN_WIDTH = N_HEADS * HEAD_DIM
KV_WIDTH = N_KV_HEADS * HEAD_DIM
N_IDX_HEADS = 8
IDX_DIM = 64
TOPK_MAX = 256
SSM_GROUP = 16
SSM_STATE = 64
TOP_K_EXPERTS = 2
LN_EPS = 1e-5
NEG_BIG = -1e30

LANES = 128
SUBLANES = 8
VMEM_LIMIT_BYTES = 56 * 1024 * 1024

TOKEN_TILE = 256
Q_TILE = 128
KEY_CHUNK = 512
SSM_CHUNK = 16
SSM_ROW_TILE = 128
SSM_GROUPS_PER_BLOCK = LANES // SSM_GROUP
SSM_BLOCK_STATES = SSM_GROUPS_PER_BLOCK * SSM_STATE
PAGES_PER_STEP = 8

INT_MIN = -(2 ** 31)
KEY_NEG_INF = int(np.int32(np.uint32(0xFF800000)) ^ np.int32(0x7FFFFFFF))


def _cparams(*sem):
    return pltpu.CompilerParams(dimension_semantics=sem, vmem_limit_bytes=VMEM_LIMIT_BYTES)


def _resident(shape):
    nd = len(shape)
    return pl.BlockSpec(tuple(shape), lambda *_: (0,) * nd, pipeline_mode=pl.Buffered(1))


def _row_tile(n, want):
    t = min(n, want)
    assert n % t == 0, (n, t)
    return t


def _layer_norm(x, g, b):
    mu = jnp.mean(x, axis=-1, keepdims=True)
    xc = x - mu
    var = jnp.mean(xc * xc, axis=-1, keepdims=True)
    return xc * lax.rsqrt(var + LN_EPS) * g + b


def _dot(a, b):
    return jnp.dot(a, b, preferred_element_type=F32)


def _dot_nt(a, b):
    return lax.dot_general(a, b, (((1,), (1,)), ((), ())), preferred_element_type=F32)


def _in_proj_kernel(x_ref, wq, wk, wv, wqi, wki, wwi, wu, wga, wgb,
                    q_o, k_o, v_o, kb_o, vb_o, qi_o, ki_o, kib_o, wi_o, u_o, ga_o, gb_o):
    xb = x_ref[...].astype(BF16)
    q_o[...] = _dot(xb, wq[...]).astype(BF16)
    k = _dot(xb, wk[...])
    k_o[...] = k
    kb_o[...] = k.astype(BF16)
    v = _dot(xb, wv[...])
    v_o[...] = v
    vb_o[...] = v.astype(BF16)
    qi = _dot(xb, wqi[...]).astype(BF16)
    for h in range(N_IDX_HEADS):
        qi_o[h] = qi[:, h * IDX_DIM:(h + 1) * IDX_DIM]
    ki = _dot(xb, wki[...])
    ki_o[...] = ki
    kib_o[...] = ki.astype(BF16)
    wi_o[...] = _dot(xb, wwi[...])
    u_o[...] = _dot(xb, wu[...])
    ga_o[...] = _dot(xb, wga[...])
    gb_o[...] = _dot(xb, wgb[...])


def _in_proj(x, ws):
    n, d = x.shape
    tm = _row_tile(n, TOKEN_TILE)
    widths = [w.shape[1] for w in ws]
    aw, kvw, _, qiw, kiw, wiw, uw, gw, _ = widths
    row = lambda w: pl.BlockSpec((tm, w), lambda i: (i, 0))
    out_shape = [
        jax.ShapeDtypeStruct((n, aw), BF16),
        jax.ShapeDtypeStruct((n, kvw), F32),
        jax.ShapeDtypeStruct((n, kvw), F32),
        jax.ShapeDtypeStruct((n, kvw), BF16),
        jax.ShapeDtypeStruct((n, kvw), BF16),
        jax.ShapeDtypeStruct((N_IDX_HEADS, n, IDX_DIM), BF16),
        jax.ShapeDtypeStruct((n, kiw), F32),
        jax.ShapeDtypeStruct((n, kiw), BF16),
        jax.ShapeDtypeStruct((n, wiw), F32),
        jax.ShapeDtypeStruct((n, uw), F32),
        jax.ShapeDtypeStruct((n, gw), F32),
        jax.ShapeDtypeStruct((n, gw), F32),
    ]
    out_specs = [row(aw), row(kvw), row(kvw), row(kvw), row(kvw),
                 pl.BlockSpec((N_IDX_HEADS, tm, IDX_DIM), lambda i: (0, i, 0)),
                 row(kiw), row(kiw), row(wiw), row(uw), row(gw), row(gw)]
    return pl.pallas_call(
        _in_proj_kernel,
        grid=(n // tm,),
        in_specs=[row(d)] + [_resident(w.shape) for w in ws],
        out_specs=out_specs,
        out_shape=out_shape,
        compiler_params=_cparams("parallel"),
        name="in_proj",
    )(x, *ws)


def _score_to_key(score):
    score = jnp.where(score == 0.0, 0.0, score)
    bits = lax.bitcast_convert_type(score, I32)
    return bits ^ ((bits >> 31) & 0x7FFFFFFF)


def _select_topk(keys_ref, rows, nkc, topk, idx_bits):
    ck = KEY_CHUNK
    lane = lax.broadcasted_iota(I32, (rows, ck), 1)

    def count(pred):
        def body(c, acc):
            off = pl.multiple_of(c * ck, ck)
            m = jnp.where(pred(keys_ref[:, pl.ds(off, ck)], off), 1.0, 0.0)
            part = m[:, 0:LANES]
            for t in range(1, ck // LANES):
                part = part + m[:, t * LANES:(t + 1) * LANES]
            return acc + part
        acc = lax.fori_loop(0, nkc, body, jnp.zeros((rows, LANES), F32))
        return jnp.sum(acc, axis=1, keepdims=True)

    kf = float(topk)
    t0 = jnp.where(count(lambda kc, off: kc >= 0) >= kf, 0, INT_MIN).astype(I32)

    def bit_body(j, t):
        cand = t + jnp.left_shift(jnp.int32(1), 30 - j)
        return jnp.where(count(lambda kc, off: kc >= cand) >= kf, cand, t)

    t = lax.fori_loop(0, 31, bit_body, t0)
    t = jnp.maximum(t, KEY_NEG_INF + 1)

    cnt_gt = count(lambda kc, off: kc > t)
    cnt_ge = count(lambda kc, off: kc >= t)
    need = kf - cnt_gt
    tie = (cnt_ge - cnt_gt) > need

    @pl.when(jnp.max(jnp.where(tie, 1.0, 0.0)) > 0.0)
    def _():
        def idx_body(j, m):
            cand = m + jnp.left_shift(jnp.int32(1), idx_bits - 1 - j)
            c = count(lambda kc, off: jnp.where(kc == t, off + lane, cand) < cand)
            return jnp.where(c < need, cand, m)

        m = lax.fori_loop(0, idx_bits, idx_body, jnp.zeros((rows, 1), I32))

        def fix(c, carry):
            off = pl.multiple_of(c * ck, ck)
            kc = keys_ref[:, pl.ds(off, ck)]
            lose = jnp.where(kc == t, off + lane, -1) > jnp.where(tie, m, 2 ** 30)
            keys_ref[:, pl.ds(off, ck)] = jnp.where(lose, kc - 1, kc)
            return carry

        lax.fori_loop(0, nkc, fix, 0)

    return t


def _prompt_attn_kernel(q_ref, qi_ref, wi_ref, k_ref, v_ref, ki_ref, o_ref, keys_ref, *, topk, idx_bits):
    tq, ck = Q_TILE, KEY_CHUNK
    i = pl.program_id(1)
    row0 = i * tq
    nkc = (row0 + tq + ck - 1) // ck
    row = row0 + lax.broadcasted_iota(I32, (tq, ck), 0)
    lane = lax.broadcasted_iota(I32, (tq, ck), 1)
    wi = wi_ref[...]

    def score_body(c, carry):
        off = pl.multiple_of(c * ck, ck)
        kic = ki_ref[pl.ds(off, ck), :]
        acc = jnp.zeros((tq, ck), F32)
        for h in range(N_IDX_HEADS):
            d = _dot_nt(qi_ref[h], kic)
            acc = acc + wi[:, h:h + 1] * jnp.maximum(d, 0.0)
        acc = jnp.where(off + lane <= row, acc, -jnp.inf)
        keys_ref[:, pl.ds(off, ck)] = _score_to_key(acc)
        return carry

    lax.fori_loop(0, nkc, score_body, 0)
    t = _select_topk(keys_ref, tq, nkc, topk, idx_bits)

    scale = HEAD_DIM ** -0.5
    for g in range(N_KV_HEADS):
        qs = [q_ref[:, (g * KV_GROUP + j) * HEAD_DIM:(g * KV_GROUP + j + 1) * HEAD_DIM]
              for j in range(KV_GROUP)]

        def attn_body(c, carry, g=g, qs=qs):
            off = pl.multiple_of(c * ck, ck)
            kc = k_ref[pl.ds(off, ck), g * HEAD_DIM:(g + 1) * HEAD_DIM]
            vc = v_ref[pl.ds(off, ck), g * HEAD_DIM:(g + 1) * HEAD_DIM]
            sel = keys_ref[:, pl.ds(off, ck)] >= t
            out = []
            for j in range(KV_GROUP):
                m, l, acc = carry[j]
                s = jnp.where(sel, _dot_nt(qs[j], kc) * scale, NEG_BIG)
                m_new = jnp.maximum(m, jnp.max(s, axis=1, keepdims=True))
                p = jnp.where(sel, jnp.exp(s - m_new), 0.0)
                alpha = jnp.exp(m - m_new)
                l = alpha * l + jnp.sum(p, axis=1, keepdims=True)
                acc = alpha * acc + _dot(p.astype(BF16), vc)
                out.append((m_new, l, acc))
            return tuple(out)

        init = tuple((jnp.full((tq, 1), NEG_BIG, F32), jnp.zeros((tq, 1), F32),
                      jnp.zeros((tq, HEAD_DIM), F32)) for _ in range(KV_GROUP))
        res = lax.fori_loop(0, nkc, attn_body, init)
        for j in range(KV_GROUP):
            _, l, acc = res[j]
            h = g * KV_GROUP + j
            o_ref[:, h * HEAD_DIM:(h + 1) * HEAD_DIM] = (acc / l).astype(o_ref.dtype)


def _prompt_attention(q, qi, wi, kb, vb, kib, batch, seq):
    topk = min(TOPK_MAX, seq // 4)
    assert seq % KEY_CHUNK == 0 and seq % Q_TILE == 0
    idx_bits = max(1, (seq - 1).bit_length())
    q3 = q.reshape(batch, seq, ATTN_WIDTH)
    qi4 = qi.reshape(N_IDX_HEADS, batch, seq, IDX_DIM)
    wi3 = wi.reshape(batch, seq, N_IDX_HEADS)
    k3 = kb.reshape(batch, seq, KV_WIDTH)
    v3 = vb.reshape(batch, seq, KV_WIDTH)
    ki3 = kib.reshape(batch, seq, IDX_DIM)
    tq = Q_TILE
    out = pl.pallas_call(
        functools.partial(_prompt_attn_kernel, topk=topk, idx_bits=idx_bits),
        grid=(batch, seq // tq),
        in_specs=[
            pl.BlockSpec((None, tq, ATTN_WIDTH), lambda b, i: (b, i, 0)),
            pl.BlockSpec((N_IDX_HEADS, None, tq, IDX_DIM), lambda b, i: (0, b, i, 0)),
            pl.BlockSpec((None, tq, N_IDX_HEADS), lambda b, i: (b, i, 0)),
            pl.BlockSpec((None, seq, KV_WIDTH), lambda b, i: (b, 0, 0)),
            pl.BlockSpec((None, seq, KV_WIDTH), lambda b, i: (b, 0, 0)),
            pl.BlockSpec((None, seq, IDX_DIM), lambda b, i: (b, 0, 0)),
        ],
        out_specs=pl.BlockSpec((None, tq, ATTN_WIDTH), lambda b, i: (b, i, 0)),
        out_shape=jax.ShapeDtypeStruct((batch, seq, ATTN_WIDTH), BF16),
        scratch_shapes=[pltpu.VMEM((tq, seq), I32)],
        compiler_params=_cparams("parallel", "arbitrary"),
        name="prompt_attention",
    )(q3, qi4, wi3, k3, v3, ki3)
    return out.reshape(batch * seq, ATTN_WIDTH)


def _sample_scores_kernel(pt_ref, qi_ref, wi_ref, kin_ref, *rest, n_groups):
    page_refs, keys_ref = rest[:PAGES_PER_STEP], rest[PAGES_PER_STEP]
    j = pl.program_id(1)
    qi = qi_ref[...]
    wi = wi_ref[...]

    @pl.when(j < n_groups)
    def _():
        for p in range(PAGES_PER_STEP):
            d = _dot_nt(qi, page_refs[p][...].astype(BF16))
            s = jnp.sum(wi * jnp.maximum(d, 0.0), axis=0, keepdims=True)
            keys_ref[p:p + 1, :] = _score_to_key(s)

    @pl.when(j == n_groups)
    def _():
        kin = kin_ref[...].astype(BF16).astype(F32)
        d = jnp.sum(qi.astype(F32) * kin, axis=1, keepdims=True)
        s = jnp.sum(wi * jnp.maximum(d, 0.0), axis=0, keepdims=True)
        first = (lax.broadcasted_iota(I32, keys_ref.shape, 0) == 0) & (
            lax.broadcasted_iota(I32, keys_ref.shape, 1) == 0)
        keys_ref[...] = jnp.where(first, _score_to_key(s), KEY_NEG_INF)


def _sample_select_kernel(keys_in_ref, mask_ref, keys_ref, *, topk, idx_bits):
    rows, width = keys_in_ref.shape
    keys_ref[...] = keys_in_ref[...]
    t = _select_topk(keys_ref, rows, width // KEY_CHUNK, topk, idx_bits)
    mask_ref[...] = jnp.where(keys_ref[...] >= t, 1.0, 0.0)


def _sample_attend_kernel(pt_ref, q_ref, mask_ref, mask_new_ref, kn_ref, vn_ref, *rest, n_groups):
    np_ = PAGES_PER_STEP
    k_refs, v_refs = rest[:np_], rest[np_:2 * np_]
    o_ref, m_ref, l_ref, acc_ref = rest[2 * np_:]
    j = pl.program_id(1)
    scale = HEAD_DIM ** -0.5
    q = q_ref[...]
    head_group = lax.broadcasted_iota(I32, (N_HEADS, 1), 0) // KV_GROUP

    @pl.when(j == 0)
    def _():
        m_ref[...] = jnp.full(m_ref.shape, NEG_BIG, F32)
        l_ref[...] = jnp.zeros(l_ref.shape, F32)
        acc_ref[...] = jnp.zeros(acc_ref.shape, F32)

    def update(s, sel, pv_fn):
        s = jnp.where(sel, s * scale, NEG_BIG)
        m_old = m_ref[...]
        m_new = jnp.maximum(m_old, jnp.max(s, axis=1, keepdims=True))
        p = jnp.where(sel, jnp.exp(s - m_new), 0.0)
        alpha = jnp.exp(m_old - m_new)
        l_ref[...] = alpha * l_ref[...] + jnp.sum(p, axis=1, keepdims=True)
        acc_ref[...] = alpha * acc_ref[...] + pv_fn(p)
        m_ref[...] = m_new

    for pg in range(np_):
        kp = k_refs[pg][...].astype(BF16)
        vp = v_refs[pg][...].astype(BF16)
        sel = mask_ref[pg:pg + 1, :] > 0.0
        s = jnp.zeros((N_HEADS, kp.shape[0]), F32)
        for g in range(N_KV_HEADS):
            sg = _dot_nt(q, kp[:, g * HEAD_DIM:(g + 1) * HEAD_DIM])
            s = jnp.where(head_group == g, sg, s)

        def pv(p, vp=vp):
            pb = p.astype(BF16)
            acc = jnp.zeros((N_HEADS, HEAD_DIM), F32)
            for g in range(N_KV_HEADS):
                og = _dot(pb, vp[:, g * HEAD_DIM:(g + 1) * HEAD_DIM])
                acc = jnp.where(head_group == g, og, acc)
            return acc

        update(s, sel, pv)

    @pl.when(j == n_groups - 1)
    def _():
        kn = kn_ref[...].astype(BF16).astype(F32)
        vn = vn_ref[...].astype(BF16).astype(F32)
        qf = q.astype(F32)
        s = jnp.zeros((N_HEADS, 1), F32)
        vsel = jnp.zeros((N_HEADS, HEAD_DIM), F32)
        for g in range(N_KV_HEADS):
            sl = slice(g * HEAD_DIM, (g + 1) * HEAD_DIM)
            sg = jnp.sum(qf * kn[:, sl], axis=1, keepdims=True)
            s = jnp.where(head_group == g, sg, s)
            vsel = jnp.where(head_group == g, vn[:, sl], vsel)
        sel = mask_new_ref[0:1, 0:1] > 0.0
        update(s, sel, lambda p: p.astype(BF16).astype(F32) * vsel)
        o_ref[...] = (acc_ref[...] / l_ref[...]).astype(o_ref.dtype)


def _sample_attention(q, qi, wi, k_new, v_new, ki_new, cache_k, cache_v, cache_kidx, page_table, layer):
    db, n_pages = page_table.shape
    page = cache_k.shape[2]
    npg = PAGES_PER_STEP
    assert n_pages % npg == 0 and page == LANES
    n_groups = n_pages // npg
    past = n_pages * page
    topk = min(TOPK_MAX, (past + 1) // 4)
    n_rows = n_pages + npg
    width = n_rows * page
    assert width % KEY_CHUNK == 0
    idx_bits = (width - 1).bit_length()
    pt = page_table.reshape(-1).astype(I32)

    ck4 = cache_k.reshape(cache_k.shape[0], cache_k.shape[1], page, KV_WIDTH)
    cv4 = cache_v.reshape(cache_v.shape[0], cache_v.shape[1], page, KV_WIDTH)
    qi3 = jnp.transpose(qi, (1, 0, 2))
    wi3 = wi.reshape(db, N_IDX_HEADS, 1)
    q3 = q.reshape(db, N_HEADS, HEAD_DIM)

    def page_spec(width_, p):
        def imap(b, j, pt_ref):
            pg = jnp.minimum(j * npg + p, n_pages - 1)
            return (layer, pt_ref[b * n_pages + pg], 0, 0)
        return pl.BlockSpec((None, None, page, width_), imap)

    keys = pl.pallas_call(
        functools.partial(_sample_scores_kernel, n_groups=n_groups),
        grid_spec=pltpu.PrefetchScalarGridSpec(
            num_scalar_prefetch=1,
            grid=(db, n_groups + 1),
            in_specs=[
                pl.BlockSpec((None, N_IDX_HEADS, IDX_DIM), lambda b, j, pt_ref: (b, 0, 0)),
                pl.BlockSpec((None, N_IDX_HEADS, 1), lambda b, j, pt_ref: (b, 0, 0)),
                pl.BlockSpec((None, 1, IDX_DIM), lambda b, j, pt_ref: (b, 0, 0)),
            ] + [page_spec(IDX_DIM, p) for p in range(npg)],
            out_specs=pl.BlockSpec((None, npg, page), lambda b, j, pt_ref: (b, j, 0)),
        ),
        out_shape=jax.ShapeDtypeStruct((db, n_rows, page), I32),
        compiler_params=_cparams("parallel", "arbitrary"),
        name="sample_scores",
    )(pt, qi3, wi3, ki_new.reshape(db, 1, IDX_DIM), *([cache_kidx] * npg))

    mask = pl.pallas_call(
        functools.partial(_sample_select_kernel, topk=topk, idx_bits=idx_bits),
        out_shape=jax.ShapeDtypeStruct((db, width), F32),
        scratch_shapes=[pltpu.VMEM((db, width), I32)],
        compiler_params=pltpu.CompilerParams(vmem_limit_bytes=VMEM_LIMIT_BYTES),
        name="sample_select",
    )(keys.reshape(db, width))
    mask3 = mask.reshape(db, n_rows, page)

    out = pl.pallas_call(
        functools.partial(_sample_attend_kernel, n_groups=n_groups),
        grid_spec=pltpu.PrefetchScalarGridSpec(
            num_scalar_prefetch=1,
            grid=(db, n_groups),
            in_specs=[
                pl.BlockSpec((None, N_HEADS, HEAD_DIM), lambda b, j, pt_ref: (b, 0, 0)),
                pl.BlockSpec((None, npg, page), lambda b, j, pt_ref: (b, j, 0)),
                pl.BlockSpec((None, npg, page), lambda b, j, pt_ref: (b, n_groups, 0)),
                pl.BlockSpec((None, 1, KV_WIDTH), lambda b, j, pt_ref: (b, 0, 0)),
                pl.BlockSpec((None, 1, KV_WIDTH), lambda b, j, pt_ref: (b, 0, 0)),
            ] + [page_spec(KV_WIDTH, p) for p in range(npg)] * 2,
            out_specs=pl.BlockSpec((None, N_HEADS, HEAD_DIM), lambda b, j, pt_ref: (b, 0, 0)),
            scratch_shapes=[pltpu.VMEM((N_HEADS, 1), F32), pltpu.VMEM((N_HEADS, 1), F32),
                            pltpu.VMEM((N_HEADS, HEAD_DIM), F32)],
        ),
        out_shape=jax.ShapeDtypeStruct((db, N_HEADS, HEAD_DIM), BF16),
        compiler_params=_cparams("parallel", "arbitrary"),
        name="sample_attend",
    )(pt, q3, mask3, mask3, k_new.reshape(db, 1, KV_WIDTH), v_new.reshape(db, 1, KV_WIDTH),
      *([ck4] * npg), *([cv4] * npg))
    return out.reshape(db, ATTN_WIDTH)


def _s5_scan_kernel(u_ref, h0_ref, a_ref, bbd_ref, cbd_ref, d_ref, wglu_ref, bglu_ref,
                    *out_refs, n_steps, emit_y):
    if emit_y:
        y_ref, hend_ref = out_refs
    else:
        (hend_ref,) = out_refs
    width = SSM_GROUPS_PER_BLOCK * SSM_GROUP * (bbd_ref.shape[0])
    nb = bbd_ref.shape[0]
    bs = SSM_BLOCK_STATES
    h = [None] * nb
    for blk in range(nb):
        h[blk] = (h0_ref[:, blk * 2 * bs:blk * 2 * bs + bs], h0_ref[:, blk * 2 * bs + bs:(blk + 1) * 2 * bs])
    for tau in range(n_steps):
        u = u_ref[:, tau * width:(tau + 1) * width]
        ub = u.astype(BF16)
        ys = []
        for blk in range(nb):
            bu = _dot(ub[:, blk * LANES:(blk + 1) * LANES], bbd_ref[blk])
            ar = a_ref[0:1, blk * 2 * bs:blk * 2 * bs + bs]
            ai = a_ref[0:1, blk * 2 * bs + bs:(blk + 1) * 2 * bs]
            hr, hi = h[blk]
            nr = ar * hr - ai * hi + bu[:, :bs]
            ni = ar * hi + ai * hr + bu[:, bs:]
            h[blk] = (nr, ni)
            if emit_y:
                ys.append(_dot(nr.astype(BF16), cbd_ref[blk, :bs, :])
                          - _dot(ni.astype(BF16), cbd_ref[blk, bs:, :]))
        if emit_y:
            y = jnp.concatenate(ys, axis=1) + d_ref[...] * u
            y = jax.nn.gelu(y)
            z = _dot(y.astype(BF16), wglu_ref[...]) + bglu_ref[...]
            y_ref[:, tau * width:(tau + 1) * width] = (y * jax.nn.sigmoid(z)).astype(y_ref.dtype)
    for blk in range(nb):
        hend_ref[:, blk * 2 * bs:blk * 2 * bs + bs] = h[blk][0]
        hend_ref[:, blk * 2 * bs + bs:(blk + 1) * 2 * bs] = h[blk][1]


def _s5_scan(u_rows, h0, p, n_steps, emit_y):
    r = u_rows.shape[0]
    tr = _row_tile(r, SSM_ROW_TILE)
    hw = h0.shape[1]
    row = lambda w: pl.BlockSpec((tr, w), lambda i: (i, 0))
    out_shape = [jax.ShapeDtypeStruct((r, hw), F32)]
    out_specs = [row(hw)]
    if emit_y:
        out_shape = [jax.ShapeDtypeStruct(u_rows.shape, BF16)] + out_shape
        out_specs = [row(u_rows.shape[1])] + out_specs
    consts = [p["a"], p["bbd"], p["cbd"], p["d"], p["wglu"], p["bglu"]]
    return pl.pallas_call(
        functools.partial(_s5_scan_kernel, n_steps=n_steps, emit_y=emit_y),
        grid=(r // tr,),
        in_specs=[row(u_rows.shape[1]), row(hw)] + [_resident(c.shape) for c in consts],
        out_specs=out_specs,
        out_shape=out_shape,
        compiler_params=_cparams("parallel"),
        name="s5_scan_y" if emit_y else "s5_scan_state",
    )(u_rows, h0, *consts)


def _s5_carry_kernel(s_ref, a_ref, hprev_ref, hend_ref, *, n_chunks, chunk_len):
    hw = s_ref.shape[1]
    piece = SSM_BLOCK_STATES
    for blk in range(hw // (2 * piece)):
        re = pl.ds(blk * 2 * piece, piece)
        im = pl.ds(blk * 2 * piece + piece, piece)
        ar, ai = a_ref[0:1, re], a_ref[0:1, im]
        pr, pi = ar, ai
        for _ in range(chunk_len - 1):
            pr, pi = pr * ar - pi * ai, pr * ai + pi * ar

        def body(c, carry, re=re, im=im, pr=pr, pi=pi):
            hr, hi = carry
            hprev_ref[pl.ds(c, 1), re] = hr
            hprev_ref[pl.ds(c, 1), im] = hi
            sr = s_ref[pl.ds(c, 1), re]
            si = s_ref[pl.ds(c, 1), im]
            return pr * hr - pi * hi + sr, pr * hi + pi * hr + si

        zero = jnp.zeros((1, piece), F32)
        hr, hi = lax.fori_loop(0, n_chunks, body, (zero, zero))
        hend_ref[0:1, re] = hr
        hend_ref[0:1, im] = hi


def _s5_carry(s, a, batch, n_chunks, chunk_len):
    hw = s.shape[1]
    s3 = s.reshape(batch, n_chunks, hw)
    hprev, hend = pl.pallas_call(
        functools.partial(_s5_carry_kernel, n_chunks=n_chunks, chunk_len=chunk_len),
        grid=(batch,),
        in_specs=[pl.BlockSpec((None, n_chunks, hw), lambda b: (b, 0, 0)), _resident(a.shape)],
        out_specs=[pl.BlockSpec((None, n_chunks, hw), lambda b: (b, 0, 0)),
                   pl.BlockSpec((None, 1, hw), lambda b: (b, 0, 0))],
        out_shape=[jax.ShapeDtypeStruct((batch, n_chunks, hw), F32),
                   jax.ShapeDtypeStruct((batch, 1, hw), F32)],
        compiler_params=_cparams("parallel"),
        name="s5_carry",
    )(s3, a)
    return hprev.reshape(batch * n_chunks, hw), hend.reshape(batch, hw)


def _s5_params(a_re, a_im, log_dt, b_re, b_im, c_re, c_im, d_skip, w_glu, b_glu):
    g, p = a_re.shape
    c = b_re.shape[2]
    gpb = SSM_GROUPS_PER_BLOCK
    nb = g // gpb
    lr, li = a_re.astype(F32), a_im.astype(F32)
    dt = jnp.exp(log_dt.astype(F32))[:, None]
    mag = jnp.exp(lr * dt)
    ab_re, ab_im = mag * jnp.cos(li * dt), mag * jnp.sin(li * dt)
    den = lr * lr + li * li
    nr = ab_re - 1.0
    f_re = (nr * lr + ab_im * li) / den
    f_im = (ab_im * lr - nr * li) / den
    br, bi = b_re.astype(F32), b_im.astype(F32)
    bb_re = f_re[..., None] * br - f_im[..., None] * bi
    bb_im = f_re[..., None] * bi + f_im[..., None] * br
    eye = jnp.eye(gpb, dtype=F32)

    def state_layout(re, im):
        return jnp.concatenate([re.reshape(nb, gpb * p), im.reshape(nb, gpb * p)], axis=1).reshape(1, -1)

    def in_block(x):
        x = jnp.transpose(x.reshape(nb, gpb, p, c), (0, 1, 3, 2))
        return (x[:, :, :, None, :] * eye[None, :, None, :, None]).reshape(nb, gpb * c, gpb * p)

    def out_block(x):
        x = jnp.transpose(x.reshape(nb, gpb, c, p), (0, 1, 3, 2))
        return (x[:, :, :, None, :] * eye[None, :, None, :, None]).reshape(nb, gpb * p, gpb * c)

    bbd = jnp.concatenate([in_block(bb_re), in_block(bb_im)], axis=2).astype(BF16)
    cbd = jnp.concatenate([out_block(c_re.astype(F32)), out_block(c_im.astype(F32))], axis=1).astype(BF16)
    return {"a": state_layout(ab_re, ab_im), "bbd": bbd, "cbd": cbd,
            "d": d_skip.astype(F32).reshape(1, -1), "wglu": w_glu.astype(BF16),
            "bglu": b_glu.astype(F32).reshape(1, -1)}


def _to_state_layout(re, im):
    b, g, p = re.shape
    nb = g // SSM_GROUPS_PER_BLOCK
    return jnp.concatenate([re.reshape(b, nb, -1), im.reshape(b, nb, -1)], axis=2).reshape(b, -1)


def _from_state_layout(h, g, p):
    b = h.shape[0]
    nb = g // SSM_GROUPS_PER_BLOCK
    h4 = h.reshape(b, nb, 2, SSM_GROUPS_PER_BLOCK * p)
    return h4[:, :, 0].reshape(b, g, p), h4[:, :, 1].reshape(b, g, p)


def _s5_prompt(u, p, batch, seq, g, n_state):
    w = u.shape[1]
    lc = SSM_CHUNK
    assert seq % lc == 0
    n_chunks = seq // lc
    rows = batch * n_chunks
    u_rows = u.reshape(rows, lc * w)
    zeros = jnp.zeros((rows, p["a"].shape[1]), F32)
    (s_loc,) = _s5_scan(u_rows, zeros, p, lc, emit_y=False)
    hprev, hend = _s5_carry(s_loc, p["a"], batch, n_chunks, lc)
    y, _ = _s5_scan(u_rows, hprev, p, lc, emit_y=True)
    hr, hi = _from_state_layout(hend, g, n_state)
    return y.reshape(batch * seq, w), hr, hi


def _s5_sample(u, h0_re, h0_im, p):
    g, n_state = h0_re.shape[1:]
    y, hend = _s5_scan(u, _to_state_layout(h0_re.astype(F32), h0_im.astype(F32)), p, 1, emit_y=True)
    hr, hi = _from_state_layout(hend, g, n_state)
    return y, hr, hi


def _mix_kernel(x_ref, attn_ref, ssm_ref, ga_ref, gb_ref, wa_ref, ws_ref, wo_ref, g_ref, b_ref, o_ref, *, alpha):
    a = _dot(attn_ref[...], wa_ref[...])
    s = _dot(ssm_ref[...], ws_ref[...])
    merged = jax.nn.sigmoid(ga_ref[...]) * a + jax.nn.sigmoid(gb_ref[...]) * s
    mix = _dot(merged.astype(BF16), wo_ref[...])
    o_ref[...] = _layer_norm(alpha * x_ref[...] + mix, g_ref[...], b_ref[...])


def _mix(x, attn, ssm, ga, gb, wa, ws, wo, g, b, alpha):
    n, d = x.shape
    tm = _row_tile(n, TOKEN_TILE)
    row = lambda w: pl.BlockSpec((tm, w), lambda i: (i, 0))
    return pl.pallas_call(
        functools.partial(_mix_kernel, alpha=alpha),
        grid=(n // tm,),
        in_specs=[row(d), row(attn.shape[1]), row(ssm.shape[1]), row(d), row(d),
                  _resident(wa.shape), _resident(ws.shape), _resident(wo.shape),
                  _resident(g.shape), _resident(b.shape)],
        out_specs=row(d),
        out_shape=jax.ShapeDtypeStruct((n, d), F32),
        compiler_params=_cparams("parallel"),
        name="mix_out_ln",
    )(x, attn, ssm, ga, gb, wa, ws, wo, g, b)


def _ffn_kernel(x_ref, w1_ref, w3_ref, w2_ref, g_ref, b_ref, o_ref, *, alpha, f_chunk):
    x = x_ref[...]
    xb = x.astype(BF16)
    acc = jnp.zeros(x.shape, F32)
    for c in range(w1_ref.shape[1] // f_chunk):
        sl = slice(c * f_chunk, (c + 1) * f_chunk)
        h = jax.nn.silu(_dot(xb, w1_ref[:, sl])) * _dot(xb, w3_ref[:, sl])
        acc = acc + _dot(h.astype(BF16), w2_ref[sl, :])
    o_ref[...] = _layer_norm(alpha * x + acc, g_ref[...], b_ref[...])


def _ffn_chunk(d_ff):
    best = LANES
    for c in range(LANES, d_ff + 1, LANES):
        if d_ff % c == 0 and c <= 1536:
            best = c
    return best


def _ffn(x, w1, w3, w2, g, b, alpha):
    n, d = x.shape
    tm = _row_tile(n, TOKEN_TILE)
    row = pl.BlockSpec((tm, d), lambda i: (i, 0))
    return pl.pallas_call(
        functools.partial(_ffn_kernel, alpha=alpha, f_chunk=_ffn_chunk(w1.shape[1])),
        grid=(n // tm,),
        in_specs=[row, _resident(w1.shape), _resident(w3.shape), _resident(w2.shape),
                  _resident(g.shape), _resident(b.shape)],
        out_specs=row,
        out_shape=jax.ShapeDtypeStruct((n, d), F32),
        compiler_params=_cparams("parallel"),
        name="ffn_ln",
    )(x, w1, w3, w2, g, b)


def _moe_kernel(x_ref, wr_ref, br_ref, w1_ref, w3_ref, w2_ref, g_ref, b_ref, o_ref, comb_ref, acc_ref, *, alpha):
    e = pl.program_id(1)
    x = x_ref[...]
    xb = x.astype(BF16)
    n_exp = comb_ref.shape[1]
    lane = lax.broadcasted_iota(I32, comb_ref.shape, 1).astype(F32)

    @pl.when(e == 0)
    def _():
        logits = _dot(xb, wr_ref[...]) + br_ref[...]
        m1 = jnp.max(logits, axis=1, keepdims=True)
        i1 = jnp.min(jnp.where(logits == m1, lane, float(n_exp)), axis=1, keepdims=True)
        rest = jnp.where(lane == i1, -jnp.inf, logits)
        m2 = jnp.max(rest, axis=1, keepdims=True)
        i2 = jnp.min(jnp.where(rest == m2, lane, float(n_exp)), axis=1, keepdims=True)
        e2 = jnp.exp(m2 - m1)
        den = 1.0 + e2
        comb_ref[...] = jnp.where(lane == i1, 1.0 / den, 0.0) + jnp.where(lane == i2, e2 / den, 0.0)
        acc_ref[...] = jnp.zeros(acc_ref.shape, F32)

    h = jax.nn.silu(_dot(xb, w1_ref[...])) * _dot(xb, w3_ref[...])
    y = _dot(h.astype(BF16), w2_ref[...])
    ce = jnp.sum(jnp.where(lane == e.astype(F32), comb_ref[...], 0.0), axis=1, keepdims=True)
    acc_ref[...] = acc_ref[...] + ce * y

    @pl.when(e == n_exp - 1)
    def _():
        o_ref[...] = _layer_norm(alpha * x + acc_ref[...], g_ref[...], b_ref[...])


def _moe(x, wr, br, w1, w3, w2, g, b, alpha):
    n, d = x.shape
    n_exp, _, f = w1.shape
    tm = _row_tile(n, 2 * TOKEN_TILE)
    row = pl.BlockSpec((tm, d), lambda i, e: (i, 0))
    return pl.pallas_call(
        functools.partial(_moe_kernel, alpha=alpha),
        grid=(n // tm, n_exp),
        in_specs=[row, _resident(wr.shape), _resident(br.shape),
                  pl.BlockSpec((None, d, f), lambda i, e: (e, 0, 0)),
                  pl.BlockSpec((None, d, f), lambda i, e: (e, 0, 0)),
                  pl.BlockSpec((None, f, d), lambda i, e: (e, 0, 0)),
                  _resident(g.shape), _resident(b.shape)],
        out_specs=row,
        out_shape=jax.ShapeDtypeStruct((n, d), F32),
        scratch_shapes=[pltpu.VMEM((tm, n_exp), F32), pltpu.VMEM((tm, d), F32)],
        compiler_params=_cparams("parallel", "arbitrary"),
        name="moe_ln",
    )(x, wr, br, w1, w3, w2, g, b)


def kernel(x_prompt, x_sample, cache_k, cache_v, cache_kidx, state_ssm_re, state_ssm_im, page_table, ln1_g, ln1_b, w_in, w_attn_proj, w_ssm_proj, w_out, ssm_a_re, ssm_a_im, ssm_log_dt, ssm_b_re, ssm_b_im, ssm_c_re, ssm_c_im, ssm_d, ssm_w_glu, ssm_b_glu, ln2_g, ln2_b, ffn_w1, ffn_w3, ffn_w2, moe_w_router, moe_b_router, moe_w1, moe_w3, moe_w2):
    batch, seq, d_model = x_prompt.shape
    dec_batch, dec_seq, _ = x_sample.shape
    assert dec_seq == 1
    depth = w_in.shape[0]
    g_ssm, n_state = ssm_a_re.shape[1:]
    ssm_width = ssm_d.shape[1]
    alpha = (2.0 * depth) ** 0.25
    in_sizes = (ATTN_WIDTH, KV_WIDTH, KV_WIDTH, N_IDX_HEADS * IDX_DIM, IDX_DIM, N_IDX_HEADS,
                ssm_width, d_model, d_model)
    assert sum(in_sizes) == w_in.shape[2]
    splits = np.cumsum(np.array(in_sizes))[:-1].tolist()

    xp = x_prompt.reshape(batch * seq, d_model).astype(F32)
    xs = x_sample.reshape(dec_batch, d_model).astype(F32)
    row2 = lambda v: v.astype(F32).reshape(1, -1)

    outs = [[] for _ in range(10)]
    for l in range(depth):
        w_pieces = [w.astype(BF16) for w in jnp.split(w_in[l], splits, axis=1)]
        wa, ws, wo = w_attn_proj[l].astype(BF16), w_ssm_proj[l].astype(BF16), w_out[l].astype(BF16)
        sp = _s5_params(ssm_a_re[l], ssm_a_im[l], ssm_log_dt[l], ssm_b_re[l], ssm_b_im[l],
                        ssm_c_re[l], ssm_c_im[l], ssm_d[l], ssm_w_glu[l], ssm_b_glu[l])
        g1, b1, g2, b2 = row2(ln1_g[l]), row2(ln1_b[l]), row2(ln2_g[l]), row2(ln2_b[l])

        q, k, v, kb, vb, qi, ki, kib, wi, u, ga, gb = _in_proj(xp, w_pieces)
        attn = _prompt_attention(q, qi, wi, kb, vb, kib, batch, seq)
        ssm, hr_p, hi_p = _s5_prompt(u, sp, batch, seq, g_ssm, n_state)
        xp = _mix(xp, attn, ssm, ga, gb, wa, ws, wo, g1, b1, alpha)
        q_s, k_s, v_s, _, _, qi_s, ki_s, _, wi_s, u_s, ga_s, gb_s = _in_proj(xs, w_pieces)
        attn_s = _sample_attention(q_s, qi_s, wi_s, k_s, v_s, ki_s, cache_k, cache_v, cache_kidx,
                                   page_table, l)
        ssm_s, hr_s, hi_s = _s5_sample(u_s, state_ssm_re[l], state_ssm_im[l], sp)
        xs = _mix(xs, attn_s, ssm_s, ga_s, gb_s, wa, ws, wo, g1, b1, alpha)
        j = l // 2
        if l % 2 == 0:
            w1, w3, w2 = ffn_w1[j].astype(BF16), ffn_w3[j].astype(BF16), ffn_w2[j].astype(BF16)
            xp = _ffn(xp, w1, w3, w2, g2, b2, alpha)
            xs = _ffn(xs, w1, w3, w2, g2, b2, alpha)
        else:
            wr, br = moe_w_router[j].astype(BF16), row2(moe_b_router[j])
            w1, w3, w2 = moe_w1[j].astype(BF16), moe_w3[j].astype(BF16), moe_w2[j].astype(BF16)
            xp = _moe(xp, wr, br, w1, w3, w2, g2, b2, alpha)
            xs = _moe(xs, wr, br, w1, w3, w2, g2, b2, alpha)

        kv_shape = (batch, seq, N_KV_HEADS, HEAD_DIM)
        kvs_shape = (dec_batch, dec_seq, N_KV_HEADS, HEAD_DIM)
        for lst, val in zip(outs, (k.reshape(kv_shape), v.reshape(kv_shape), ki.reshape(batch, seq, IDX_DIM),
                                   hr_p, hi_p, k_s.reshape(kvs_shape), v_s.reshape(kvs_shape),
                                   ki_s.reshape(dec_batch, dec_seq, IDX_DIM), hr_s, hi_s)):
            lst.append(val)

    return (xp.reshape(batch, seq, d_model), xs.reshape(dec_batch, dec_seq, d_model),
            *[jnp.stack(o) for o in outs])
```

```python
import functools
import math

import jax
import jax.numpy as jnp
import numpy as np
from jax import lax
from jax.experimental import pallas as pl
from jax.experimental.pallas import tpu as pltpu

F32 = jnp.float32
BF16 = jnp.bfloat16
I32 = jnp.int32

N_HEADS = 8
HEAD_DIM = 128
N_KV_HEADS = 4
KV_GROUP = N_HEADS // N_KV_HEADS
ATTN_WIDTH = N_HEADS * HEAD_DIM
KV_WIDTH = N_KV_HEADS * HEAD_DIM
N_IDX_HEADS = 8
IDX_DIM = 64
TOPK_MAX = 256
SSM_GROUP = 16
SSM_STATE = 64
TOP_K_EXPERTS = 2
LN_EPS = 1e-5
NEG_BIG = -1e30

LANES = 128
SUBLANES = 8
VMEM_LIMIT_BYTES = 56 * 1024 * 1024

TOKEN_TILE = 256
Q_TILE = 128
KEY_CHUNK = 512
SSM_CHUNK = 16
SSM_ROW_TILE = 128
SSM_GROUPS_PER_BLOCK = LANES // SSM_GROUP
SSM_BLOCK_STATES = SSM_GROUPS_PER_BLOCK * SSM_STATE
PAGES_PER_STEP = 8

INT_MIN = -(2 ** 31)
KEY_NEG_INF = int(np.int32(np.uint32(0xFF800000)) ^ np.int32(0x7FFFFFFF))


def _cparams(*sem):
    return pltpu.CompilerParams(dimension_semantics=sem, vmem_limit_bytes=VMEM_LIMIT_BYTES)


def _resident(shape):
    nd = len(shape)
    return pl.BlockSpec(tuple(shape), lambda *_: (0,) * nd, pipeline_mode=pl.Buffered(1))


def _row_tile(n, want):
    t = min(n, want)
    assert n % t == 0, (n, t)
    return t


def _layer_norm(x, g, b):
    mu = jnp.mean(x, axis=-1, keepdims=True)
    xc = x - mu
    var = jnp.mean(xc * xc, axis=-1, keepdims=True)
    return xc * lax.rsqrt(var + LN_EPS) * g + b


def _dot(a, b):
    return jnp.dot(a, b, preferred_element_type=F32)


def _dot_nt(a, b):
    return lax.dot_general(a, b, (((1,), (1,)), ((), ())), preferred_element_type=F32)


def _in_proj_kernel(x_ref, wq, wk, wv, wqi, wki, wwi, wu, wga, wgb,
                    q_o, k_o, v_o, kb_o, vb_o, qi_o, ki_o, kib_o, wi_o, u_o, ga_o, gb_o):
    xb = x_ref[...].astype(BF16)
    q = _dot(xb, wq[...]).astype(BF16)
    for h in range(N_HEADS):
        q_o[h] = q[:, h * HEAD_DIM:(h + 1) * HEAD_DIM]
    k = _dot(xb, wk[...])
    k_o[...] = k
    kb_o[...] = k.astype(BF16)
    v = _dot(xb, wv[...])
    v_o[...] = v
    vb = v.astype(BF16)
    for g in range(N_KV_HEADS):
        vb_o[g] = vb[:, g * HEAD_DIM:(g + 1) * HEAD_DIM]
    qi = _dot(xb, wqi[...]).astype(BF16)
    for h in range(N_IDX_HEADS):
        qi_o[h] = qi[:, h * IDX_DIM:(h + 1) * IDX_DIM]
    ki = _dot(xb, wki[...])
    ki_o[...] = ki
    kib_o[...] = ki.astype(BF16)
    wi_o[...] = _dot(xb, wwi[...])
    u_o[...] = _dot(xb, wu[...])
    ga_o[...] = _dot(xb, wga[...])
    gb_o[...] = _dot(xb, wgb[...])


def _in_proj(x, ws):
    n, d = x.shape
    tm = _row_tile(n, TOKEN_TILE)
    widths = [w.shape[1] for w in ws]
    aw, kvw, _, qiw, kiw, wiw, uw, gw, _ = widths
    row = lambda w: pl.BlockSpec((tm, w), lambda i: (i, 0))
    out_shape = [
        jax.ShapeDtypeStruct((N_HEADS, n, HEAD_DIM), BF16),
        jax.ShapeDtypeStruct((n, kvw), F32),
        jax.ShapeDtypeStruct((n, kvw), F32),
        jax.ShapeDtypeStruct((n, kvw), BF16),
        jax.ShapeDtypeStruct((N_KV_HEADS, n, HEAD_DIM), BF16),
        jax.ShapeDtypeStruct((N_IDX_HEADS, n, IDX_DIM), BF16),
        jax.ShapeDtypeStruct((n, kiw), F32),
        jax.ShapeDtypeStruct((n, kiw), BF16),
        jax.ShapeDtypeStruct((n, wiw), F32),
        jax.ShapeDtypeStruct((n, uw), F32),
        jax.ShapeDtypeStruct((n, gw), F32),
        jax.ShapeDtypeStruct((n, gw), F32),
    ]
    out_specs = [pl.BlockSpec((N_HEADS, tm, HEAD_DIM), lambda i: (0, i, 0)),
                 row(kvw), row(kvw), row(kvw),
                 pl.BlockSpec((N_KV_HEADS, tm, HEAD_DIM), lambda i: (0, i, 0)),
                 pl.BlockSpec((N_IDX_HEADS, tm, IDX_DIM), lambda i: (0, i, 0)),
                 row(kiw), row(kiw), row(wiw), row(uw), row(gw), row(gw)]
    return pl.pallas_call(
        _in_proj_kernel,
        grid=(n // tm,),
        in_specs=[row(d)] + [_resident(w.shape) for w in ws],
        out_specs=out_specs,
        out_shape=out_shape,
        compiler_params=_cparams("parallel"),
        name="in_proj",
    )(x, *ws)


def _score_to_key(score):
    score = jnp.where(score == 0.0, 0.0, score)
    bits = lax.bitcast_convert_type(score, I32)
    return bits ^ ((bits >> 31) & 0x7FFFFFFF)


def _select_topk(keys_ref, rows, nkc, topk, idx_bits):
    ck = KEY_CHUNK
    lane = lax.broadcasted_iota(I32, (rows, ck), 1)

    def count(pred):
        def body(c, acc):
            off = pl.multiple_of(c * ck, ck)
            m = jnp.where(pred(keys_ref[:, pl.ds(off, ck)], off), 1.0, 0.0)
            part = m[:, 0:LANES]
            for t in range(1, ck // LANES):
                part = part + m[:, t * LANES:(t + 1) * LANES]
            return acc + part
        acc = lax.fori_loop(0, nkc, body, jnp.zeros((rows, LANES), F32))
        return jnp.sum(acc, axis=1, keepdims=True)

    kf = float(topk)
    t0 = jnp.where(count(lambda kc, off: kc >= 0) >= kf, 0, INT_MIN).astype(I32)

    def bit_body(j, t):
        cand = t + jnp.left_shift(jnp.int32(1), 30 - j)
        return jnp.where(count(lambda kc, off: kc >= cand) >= kf, cand, t)

    t = lax.fori_loop(0, 31, bit_body, t0)
    t = jnp.maximum(t, KEY_NEG_INF + 1)

    cnt_gt = count(lambda kc, off: kc > t)
    cnt_ge = count(lambda kc, off: kc >= t)
    need = kf - cnt_gt
    tie = (cnt_ge - cnt_gt) > need

    @pl.when(jnp.max(jnp.where(tie, 1.0, 0.0)) > 0.0)
    def _():
        def idx_body(j, m):
            cand = m + jnp.left_shift(jnp.int32(1), idx_bits - 1 - j)
            c = count(lambda kc, off: jnp.where(kc == t, off + lane, cand) < cand)
            return jnp.where(c < need, cand, m)

        m = lax.fori_loop(0, idx_bits, idx_body, jnp.zeros((rows, 1), I32))

        def fix(c, carry):
            off = pl.multiple_of(c * ck, ck)
            kc = keys_ref[:, pl.ds(off, ck)]
            lose = jnp.where(kc == t, off + lane, -1) > jnp.where(tie, m, 2 ** 30)
            keys_ref[:, pl.ds(off, ck)] = jnp.where(lose, kc - 1, kc)
            return carry

        lax.fori_loop(0, nkc, fix, 0)

    return t


def _prompt_attn_kernel(q_ref, qi_ref, wi_ref, kt_ref, v_ref, kit_ref, o_ref,
                        keys_ref, m_ref, l_ref, acc_ref, alpha_ref, s_ref, p_ref, *, topk, idx_bits):
    tq, ck = Q_TILE, KEY_CHUNK
    i = pl.program_id(1)
    row0 = i * tq
    nkc = (row0 + tq + ck - 1) // ck
    row = row0 + lax.broadcasted_iota(I32, (tq, ck), 0)
    lane = lax.broadcasted_iota(I32, (tq, ck), 1)
    wi = wi_ref[...]

    def score_body(c, carry):
        off = pl.multiple_of(c * ck, ck)
        kic = kit_ref[:, pl.ds(off, ck)]
        acc = jnp.zeros((tq, ck), F32)
        for h in range(N_IDX_HEADS):
            d = _dot(qi_ref[h], kic)
            acc = acc + wi[:, h:h + 1] * jnp.maximum(d, 0.0)
        acc = jnp.where(off + lane <= row, acc, -jnp.inf)
        keys_ref[:, pl.ds(off, ck)] = _score_to_key(acc)
        return carry

    lax.fori_loop(0, nkc, score_body, 0)
    t = _select_topk(keys_ref, tq, nkc, topk, idx_bits)

    c2 = (HEAD_DIM ** -0.5) * math.log2(math.e)
    m_ref[...] = jnp.full(m_ref.shape, NEG_BIG, F32)
    l_ref[...] = jnp.zeros(l_ref.shape, F32)
    acc_ref[...] = jnp.zeros(acc_ref.shape, F32)
    s_ref[...] = jnp.full(s_ref.shape, NEG_BIG, F32)
    p_ref[...] = jnp.zeros(p_ref.shape, BF16)
    alpha_ref[...] = jnp.ones(alpha_ref.shape, F32)

    n_units = nkc * N_KV_HEADS

    def attn_step(n, carry):
        unit_c = jnp.maximum(n - 2, 0)
        off_c = pl.multiple_of((unit_c // N_KV_HEADS) * ck, ck)
        g_c = (n + 2) % N_KV_HEADS
        slot_c = n % 2
        vc = v_ref[g_c, pl.ds(off_c, ck), :]
        for j in range(KV_GROUP):
            h = g_c * KV_GROUP + j
            acc_ref[h] = alpha_ref[slot_c, j] * acc_ref[h] + _dot(p_ref[slot_c, j], vc)
        g_b = (n + 3) % N_KV_HEADS
        slot_b = (n + 1) % 2
        for j in range(KV_GROUP):
            h = g_b * KV_GROUP + j
            s = s_ref[slot_b, j]
            m_old = m_ref[h]
            m_new = jnp.maximum(m_old, jnp.max(s, axis=1, keepdims=True))
            p = jnp.exp2((s - m_new) * c2)
            alpha = jnp.exp2((m_old - m_new) * c2)
            l_ref[h] = alpha * l_ref[h] + jnp.sum(p, axis=1, keepdims=True)
            p_ref[slot_b, j] = p.astype(BF16)
            alpha_ref[slot_b, j] = alpha
            m_ref[h] = m_new
        unit_a = jnp.minimum(n, n_units - 1)
        off_a = pl.multiple_of((unit_a // N_KV_HEADS) * ck, ck)
        g_a = unit_a % N_KV_HEADS
        t_n = jnp.where(n < n_units, t, 2 ** 31 - 1)
        bias = jnp.where(keys_ref[:, pl.ds(off_a, ck)] >= t_n, 0.0, NEG_BIG)
        kc = kt_ref[g_a, :, pl.ds(off_a, ck)]
        for j in range(KV_GROUP):
            s_ref[slot_c, j] = _dot(q_ref[g_a * KV_GROUP + j], kc) + bias
        return carry

    lax.fori_loop(0, n_units + 2, attn_step, 0)
    for h in range(N_HEADS):
        o_ref[:, h * HEAD_DIM:(h + 1) * HEAD_DIM] = (acc_ref[h] / l_ref[h]).astype(o_ref.dtype)


def _prompt_attention(q, qi, wi, kb, vb, kib, batch, seq):
    topk = min(TOPK_MAX, seq // 4)
    assert seq % KEY_CHUNK == 0 and seq % Q_TILE == 0
    idx_bits = max(1, (seq - 1).bit_length())
    q4 = q.reshape(N_HEADS, batch, seq, HEAD_DIM)
    qi4 = qi.reshape(N_IDX_HEADS, batch, seq, IDX_DIM)
    wi3 = wi.reshape(batch, seq, N_IDX_HEADS)
    kt4 = jnp.swapaxes(kb.reshape(batch, seq, KV_WIDTH), 1, 2).reshape(batch, N_KV_HEADS, HEAD_DIM, seq)
    v4 = vb.reshape(N_KV_HEADS, batch, seq, HEAD_DIM)
    kit3 = jnp.swapaxes(kib.reshape(batch, seq, IDX_DIM), 1, 2)
    tq = Q_TILE
    out = pl.pallas_call(
        functools.partial(_prompt_attn_kernel, topk=topk, idx_bits=idx_bits),
        grid=(batch, seq // tq),
        in_specs=[
            pl.BlockSpec((N_HEADS, None, tq, HEAD_DIM), lambda b, i: (0, b, i, 0)),
            pl.BlockSpec((N_IDX_HEADS, None, tq, IDX_DIM), lambda b, i: (0, b, i, 0)),
            pl.BlockSpec((None, tq, N_IDX_HEADS), lambda b, i: (b, i, 0)),
            pl.BlockSpec((None, N_KV_HEADS, HEAD_DIM, seq), lambda b, i: (b, 0, 0, 0)),
            pl.BlockSpec((N_KV_HEADS, None, seq, HEAD_DIM), lambda b, i: (0, b, 0, 0)),
            pl.BlockSpec((None, IDX_DIM, seq), lambda b, i: (b, 0, 0)),
        ],
        out_specs=pl.BlockSpec((None, tq, ATTN_WIDTH), lambda b, i: (b, i, 0)),
        out_shape=jax.ShapeDtypeStruct((batch, seq, ATTN_WIDTH), BF16),
        scratch_shapes=[pltpu.VMEM((tq, seq), I32),
                        pltpu.VMEM((N_HEADS, tq, 1), F32),
                        pltpu.VMEM((N_HEADS, tq, 1), F32),
                        pltpu.VMEM((N_HEADS, tq, HEAD_DIM), F32),
                        pltpu.VMEM((2, KV_GROUP, tq, 1), F32),
                        pltpu.VMEM((2, KV_GROUP, tq, KEY_CHUNK), F32),
                        pltpu.VMEM((2, KV_GROUP, tq, KEY_CHUNK), BF16)],
        compiler_params=_cparams("parallel", "arbitrary"),
        name="prompt_attention",
    )(q4, qi4, wi3, kt4, v4, kit3)
    return out.reshape(batch * seq, ATTN_WIDTH)


def _sample_scores_kernel(pt_ref, qi_ref, wi_ref, kin_ref, *rest, n_groups):
    page_refs, keys_ref = rest[:PAGES_PER_STEP], rest[PAGES_PER_STEP]
    j = pl.program_id(1)
    qi = qi_ref[...]
    wi = wi_ref[...]

    @pl.when(j < n_groups)
    def _():
        for p in range(PAGES_PER_STEP):
            d = _dot(qi, page_refs[p][...].astype(BF16))
            s = jnp.sum(wi * jnp.maximum(d, 0.0), axis=0, keepdims=True)
            keys_ref[p:p + 1, :] = _score_to_key(s)

    @pl.when(j == n_groups)
    def _():
        kin = kin_ref[...].astype(BF16).astype(F32)
        d = jnp.sum(qi.astype(F32) * kin, axis=1, keepdims=True)
        s = jnp.sum(wi * jnp.maximum(d, 0.0), axis=0, keepdims=True)
        first = (lax.broadcasted_iota(I32, keys_ref.shape, 0) == 0) & (
            lax.broadcasted_iota(I32, keys_ref.shape, 1) == 0)
        keys_ref[...] = jnp.where(first, _score_to_key(s), KEY_NEG_INF)


def _sample_select_kernel(keys_in_ref, tri_ref, digits_ref, pos_ref, cnt_ref, keys_ref, rank_ref,
                          *, topk, idx_bits, past):
    rows, width = keys_in_ref.shape
    ck = KEY_CHUNK
    keys_ref[...] = keys_in_ref[...]
    t = _select_topk(keys_ref, rows, width // ck, topk, idx_bits)

    rank_ref[...] = jnp.full(rank_ref.shape, -1.0, F32)
    carry = jnp.zeros((rows, 1), F32)
    for c in range(past // LANES):
        sl = slice(c * LANES, (c + 1) * LANES)
        m = jnp.where(keys_ref[:, sl] >= t, 1.0, 0.0)
        incl = _dot(m.astype(BF16), tri_ref[...])
        rank_ref[:, sl] = jnp.where(m > 0.0, carry + incl - 1.0, -1.0)
        carry = carry + incl[:, LANES - 1:LANES]
    cnt_ref[...] = carry

    want = lax.broadcasted_iota(I32, (topk, 1), 0).astype(F32)

    def row_body(b, carry_):
        acc = jnp.zeros((topk, LANES), F32)
        for c in range(past // ck):
            r = rank_ref[pl.ds(b, 1), c * ck:(c + 1) * ck]
            onehot = jnp.where(r == want, 1.0, 0.0).astype(BF16)
            acc = acc + _dot(onehot, digits_ref[c * ck:(c + 1) * ck, :])
        pos_ref[b] = acc
        return carry_

    lax.fori_loop(0, rows, row_body, 0)


def _sample_attend_kernel(pos_ref, cnt_ref, pt_ref, q_ref, kn_ref, vn_ref, ck_hbm, cv_hbm, o_ref,
                          kbuf, vbuf, sem, *, layer, n_pages, page, topk):
    b = pl.program_id(0)
    kvh = N_KV_HEADS

    def row_copies(bb, i, slot):
        pos = pos_ref[bb * topk + i]
        pg = pt_ref[bb * n_pages + pos // page]
        src = pl.ds(pl.multiple_of((pos % page) * kvh, kvh), kvh)
        dst = pl.ds(pl.multiple_of(i * kvh, kvh), kvh)
        return (pltpu.make_async_copy(ck_hbm.at[layer, pg, src, :], kbuf.at[slot, dst, :], sem.at[0, slot]),
                pltpu.make_async_copy(cv_hbm.at[layer, pg, src, :], vbuf.at[slot, dst, :], sem.at[1, slot]))

    def start_all(bb, slot):
        def body(i, carry):
            for cp in row_copies(bb, i, slot):
                cp.start()
            return carry
        lax.fori_loop(0, topk, body, 0)

    @pl.when(b == 0)
    def _():
        start_all(0, 0)

    @pl.when(b + 1 < pl.num_programs(0))
    def _():
        start_all(b + 1, (b + 1) % 2)

    slot = b % 2

    def wait_body(i, carry):
        for cp in row_copies(b, i, slot):
            cp.wait()
        return carry

    lax.fori_loop(0, topk, wait_body, 0)

    scale = HEAD_DIM ** -0.5
    q = q_ref[...]
    n_rows = topk * kvh
    col = lax.broadcasted_iota(I32, (N_HEADS, n_rows), 1)
    head_group = lax.broadcasted_iota(I32, (N_HEADS, n_rows), 0) // KV_GROUP
    cnt = cnt_ref[b]
    valid = (col % kvh == head_group) & (col // kvh < cnt)
    s = jnp.where(valid, _dot_nt(q, kbuf[slot].astype(BF16)) * scale, NEG_BIG)

    kn = kn_ref[...].astype(BF16).astype(F32)
    vn = vn_ref[...].astype(BF16).astype(F32)
    qf = q.astype(F32)
    row_group = lax.broadcasted_iota(I32, (N_HEADS, 1), 0) // KV_GROUP
    s_new = jnp.zeros((N_HEADS, 1), F32)
    v_new = jnp.zeros((N_HEADS, HEAD_DIM), F32)
    for g in range(kvh):
        sl = slice(g * HEAD_DIM, (g + 1) * HEAD_DIM)
        s_new = jnp.where(row_group == g, jnp.sum(qf * kn[:, sl], axis=1, keepdims=True), s_new)
        v_new = jnp.where(row_group == g, vn[:, sl], v_new)
    new_sel = cnt < topk
    s_new = jnp.where(new_sel, s_new * scale, NEG_BIG)

    m = jnp.maximum(jnp.max(s, axis=1, keepdims=True), s_new)
    p = jnp.where(valid, jnp.exp(s - m), 0.0)
    p_new = jnp.where(new_sel, jnp.exp(s_new - m), 0.0)
    l = jnp.sum(p, axis=1, keepdims=True) + p_new
    acc = _dot(p.astype(BF16), vbuf[slot].astype(BF16)) + p_new.astype(BF16).astype(F32) * v_new
    o_ref[...] = (acc / l).astype(o_ref.dtype)


def _sample_attention(q, qi, wi, k_new, v_new, ki_new, cache_k, cache_v, cache_kidx, page_table, layer):
    db, n_pages = page_table.shape
    page = cache_k.shape[2]
    npg = PAGES_PER_STEP
    assert n_pages % npg == 0 and page == LANES
    n_groups = n_pages // npg
    past = n_pages * page
    topk = min(TOPK_MAX, (past + 1) // 4)
    n_rows = n_pages + npg
    width = n_rows * page
    assert width % KEY_CHUNK == 0
    idx_bits = (width - 1).bit_length()
    pt = page_table.reshape(-1).astype(I32)

    page_rows = page * N_KV_HEADS
    ck4 = cache_k.reshape(cache_k.shape[0], cache_k.shape[1], page_rows, HEAD_DIM)
    cv4 = cache_v.reshape(cache_v.shape[0], cache_v.shape[1], page_rows, HEAD_DIM)
    kidx_t = jnp.swapaxes(cache_kidx, 2, 3)
    qi3 = jnp.transpose(qi, (1, 0, 2))
    wi3 = wi.reshape(db, N_IDX_HEADS, 1)
    q3 = jnp.transpose(q, (1, 0, 2))

    def page_spec(rows_, width_, p):
        def imap(b, j, pt_ref):
            pg = jnp.minimum(j * npg + p, n_pages - 1)
            return (layer, pt_ref[b * n_pages + pg], 0, 0)
        return pl.BlockSpec((None, None, rows_, width_), imap)

    keys = pl.pallas_call(
        functools.partial(_sample_scores_kernel, n_groups=n_groups),
        grid_spec=pltpu.PrefetchScalarGridSpec(
            num_scalar_prefetch=1,
            grid=(db, n_groups + 1),
            in_specs=[
                pl.BlockSpec((None, N_IDX_HEADS, IDX_DIM), lambda b, j, pt_ref: (b, 0, 0)),
                pl.BlockSpec((None, N_IDX_HEADS, 1), lambda b, j, pt_ref: (b, 0, 0)),
                pl.BlockSpec((None, 1, IDX_DIM), lambda b, j, pt_ref: (b, 0, 0)),
            ] + [page_spec(IDX_DIM, page, p) for p in range(npg)],
            out_specs=pl.BlockSpec((None, npg, page), lambda b, j, pt_ref: (b, j, 0)),
        ),
        out_shape=jax.ShapeDtypeStruct((db, n_rows, page), I32),
        compiler_params=_cparams("parallel", "arbitrary"),
        name="sample_scores",
    )(pt, qi3, wi3, ki_new.reshape(db, 1, IDX_DIM), *([kidx_t] * npg))

    tri = (jnp.arange(LANES)[:, None] <= jnp.arange(LANES)[None, :]).astype(BF16)
    positions = jnp.arange(past)
    digits = jnp.zeros((past, LANES), BF16).at[:, 0].set((positions // LANES).astype(BF16))
    digits = digits.at[:, 1].set((positions % LANES).astype(BF16))
    pos_digits, cnt = pl.pallas_call(
        functools.partial(_sample_select_kernel, topk=topk, idx_bits=idx_bits, past=past),
        out_shape=[jax.ShapeDtypeStruct((db, topk, LANES), F32), jax.ShapeDtypeStruct((db, 1), F32)],
        scratch_shapes=[pltpu.VMEM((db, width), I32), pltpu.VMEM((db, width), F32)],
        compiler_params=pltpu.CompilerParams(vmem_limit_bytes=VMEM_LIMIT_BYTES),
        name="sample_select",
    )(keys.reshape(db, width), tri, digits)
    pos = (pos_digits[:, :, 0] * LANES + pos_digits[:, :, 1]).astype(I32).reshape(-1)
    cnt = cnt.astype(I32).reshape(-1)

    out = pl.pallas_call(
        functools.partial(_sample_attend_kernel, layer=layer, n_pages=n_pages, page=page, topk=topk),
        grid_spec=pltpu.PrefetchScalarGridSpec(
            num_scalar_prefetch=3,
            grid=(db,),
            in_specs=[
                pl.BlockSpec((None, N_HEADS, HEAD_DIM), lambda b, *_: (b, 0, 0)),
                pl.BlockSpec((None, 1, KV_WIDTH), lambda b, *_: (b, 0, 0)),
                pl.BlockSpec((None, 1, KV_WIDTH), lambda b, *_: (b, 0, 0)),
                pl.BlockSpec(memory_space=pl.ANY),
                pl.BlockSpec(memory_space=pl.ANY),
            ],
            out_specs=pl.BlockSpec((None, N_HEADS, HEAD_DIM), lambda b, *_: (b, 0, 0)),
            scratch_shapes=[pltpu.VMEM((2, topk * N_KV_HEADS, HEAD_DIM), F32),
                            pltpu.VMEM((2, topk * N_KV_HEADS, HEAD_DIM), F32),
                            pltpu.SemaphoreType.DMA((2, 2))],
        ),
        out_shape=jax.ShapeDtypeStruct((db, N_HEADS, HEAD_DIM), BF16),
        compiler_params=_cparams("arbitrary"),
        name="sample_attend",
    )(pos, cnt, pt, q3, k_new.reshape(db, 1, KV_WIDTH), v_new.reshape(db, 1, KV_WIDTH), ck4, cv4)
    return out.reshape(db, ATTN_WIDTH)


def _s5_scan_kernel(u_ref, h0_ref, a_ref, bbd_ref, cbd_ref, d_ref, wglu_ref, bglu_ref,
                    *out_refs, n_steps, emit_y):
    if emit_y:
        y_ref, hend_ref = out_refs
    else:
        (hend_ref,) = out_refs
    width = SSM_GROUPS_PER_BLOCK * SSM_GROUP * (bbd_ref.shape[0])
    nb = bbd_ref.shape[0]
    bs = SSM_BLOCK_STATES
    h = [None] * nb
    for blk in range(nb):
        h[blk] = (h0_ref[:, blk * 2 * bs:blk * 2 * bs + bs], h0_ref[:, blk * 2 * bs + bs:(blk + 1) * 2 * bs])
    for tau in range(n_steps):
        u = u_ref[:, tau * width:(tau + 1) * width]
        ub = u.astype(BF16)
        ys = []
        for blk in range(nb):
            bu = _dot(ub[:, blk * LANES:(blk + 1) * LANES], bbd_ref[blk])
            ar = a_ref[0:1, blk * 2 * bs:blk * 2 * bs + bs]
            ai = a_ref[0:1, blk * 2 * bs + bs:(blk + 1) * 2 * bs]
            hr, hi = h[blk]
            nr = ar * hr - ai * hi + bu[:, :bs]
            ni = ar * hi + ai * hr + bu[:, bs:]
            h[blk] = (nr, ni)
            if emit_y:
                ys.append(_dot(nr.astype(BF16), cbd_ref[blk, :bs, :])
                          - _dot(ni.astype(BF16), cbd_ref[blk, bs:, :]))
        if emit_y:
            y = jnp.concatenate(ys, axis=1) + d_ref[...] * u
            y = jax.nn.gelu(y)
            z = _dot(y.astype(BF16), wglu_ref[...]) + bglu_ref[...]
            y_ref[:, tau * width:(tau + 1) * width] = (y * jax.nn.sigmoid(z)).astype(y_ref.dtype)
    for blk in range(nb):
        hend_ref[:, blk * 2 * bs:blk * 2 * bs + bs] = h[blk][0]
        hend_ref[:, blk * 2 * bs + bs:(blk + 1) * 2 * bs] = h[blk][1]


def _s5_scan(u_rows, h0, p, n_steps, emit_y):
    r = u_rows.shape[0]
    tr = _row_tile(r, SSM_ROW_TILE)
    hw = h0.shape[1]
    row = lambda w: pl.BlockSpec((tr, w), lambda i: (i, 0))
    out_shape = [jax.ShapeDtypeStruct((r, hw), F32)]
    out_specs = [row(hw)]
    if emit_y:
        out_shape = [jax.ShapeDtypeStruct(u_rows.shape, BF16)] + out_shape
        out_specs = [row(u_rows.shape[1])] + out_specs
    consts = [p["a"], p["bbd"], p["cbd"], p["d"], p["wglu"], p["bglu"]]
    return pl.pallas_call(
        functools.partial(_s5_scan_kernel, n_steps=n_steps, emit_y=emit_y),
        grid=(r // tr,),
        in_specs=[row(u_rows.shape[1]), row(hw)] + [_resident(c.shape) for c in consts],
        out_specs=out_specs,
        out_shape=out_shape,
        compiler_params=_cparams("parallel"),
        name="s5_scan_y" if emit_y else "s5_scan_state",
    )(u_rows, h0, *consts)


def _s5_carry_kernel(s_ref, a_ref, hprev_ref, hend_ref, *, n_chunks, chunk_len):
    hw = s_ref.shape[1]
    piece = SSM_BLOCK_STATES
    for blk in range(hw // (2 * piece)):
        re = pl.ds(blk * 2 * piece, piece)
        im = pl.ds(blk * 2 * piece + piece, piece)
        ar, ai = a_ref[0:1, re], a_ref[0:1, im]
        pr, pi = ar, ai
        for _ in range(chunk_len - 1):
            pr, pi = pr * ar - pi * ai, pr * ai + pi * ar

        def body(c, carry, re=re, im=im, pr=pr, pi=pi):
            hr, hi = carry
            hprev_ref[pl.ds(c, 1), re] = hr
            hprev_ref[pl.ds(c, 1), im] = hi
            sr = s_ref[pl.ds(c, 1), re]
            si = s_ref[pl.ds(c, 1), im]
            return pr * hr - pi * hi + sr, pr * hi + pi * hr + si

        zero = jnp.zeros((1, piece), F32)
        hr, hi = lax.fori_loop(0, n_chunks, body, (zero, zero))
        hend_ref[0:1, re] = hr
        hend_ref[0:1, im] = hi


def _s5_carry(s, a, batch, n_chunks, chunk_len):
    hw = s.shape[1]
    s3 = s.reshape(batch, n_chunks, hw)
    hprev, hend = pl.pallas_call(
        functools.partial(_s5_carry_kernel, n_chunks=n_chunks, chunk_len=chunk_len),
        grid=(batch,),
        in_specs=[pl.BlockSpec((None, n_chunks, hw), lambda b: (b, 0, 0)), _resident(a.shape)],
        out_specs=[pl.BlockSpec((None, n_chunks, hw), lambda b: (b, 0, 0)),
                   pl.BlockSpec((None, 1, hw), lambda b: (b, 0, 0))],
        out_shape=[jax.ShapeDtypeStruct((batch, n_chunks, hw), F32),
                   jax.ShapeDtypeStruct((batch, 1, hw), F32)],
        compiler_params=_cparams("parallel"),
        name="s5_carry",
    )(s3, a)
    return hprev.reshape(batch * n_chunks, hw), hend.reshape(batch, hw)


def _s5_params(a_re, a_im, log_dt, b_re, b_im, c_re, c_im, d_skip, w_glu, b_glu):
    g, p = a_re.shape
    c = b_re.shape[2]
    gpb = SSM_GROUPS_PER_BLOCK
    nb = g // gpb
    lr, li = a_re.astype(F32), a_im.astype(F32)
    dt = jnp.exp(log_dt.astype(F32))[:, None]
    mag = jnp.exp(lr * dt)
    ab_re, ab_im = mag * jnp.cos(li * dt), mag * jnp.sin(li * dt)
    den = lr * lr + li * li
    nr = ab_re - 1.0
    f_re = (nr * lr + ab_im * li) / den
    f_im = (ab_im * lr - nr * li) / den
    br, bi = b_re.astype(F32), b_im.astype(F32)
    bb_re = f_re[..., None] * br - f_im[..., None] * bi
    bb_im = f_re[..., None] * bi + f_im[..., None] * br
    eye = jnp.eye(gpb, dtype=F32)

    def state_layout(re, im):
        return jnp.concatenate([re.reshape(nb, gpb * p), im.reshape(nb, gpb * p)], axis=1).reshape(1, -1)

    def in_block(x):
        x = jnp.transpose(x.reshape(nb, gpb, p, c), (0, 1, 3, 2))
        return (x[:, :, :, None, :] * eye[None, :, None, :, None]).reshape(nb, gpb * c, gpb * p)

    def out_block(x):
        x = jnp.transpose(x.reshape(nb, gpb, c, p), (0, 1, 3, 2))
        return (x[:, :, :, None, :] * eye[None, :, None, :, None]).reshape(nb, gpb * p, gpb * c)

    bbd = jnp.concatenate([in_block(bb_re), in_block(bb_im)], axis=2).astype(BF16)
    cbd = jnp.concatenate([out_block(c_re.astype(F32)), out_block(c_im.astype(F32))], axis=1).astype(BF16)
    return {"a": state_layout(ab_re, ab_im), "bbd": bbd, "cbd": cbd,
            "d": d_skip.astype(F32).reshape(1, -1), "wglu": w_glu.astype(BF16),
            "bglu": b_glu.astype(F32).reshape(1, -1)}


def _to_state_layout(re, im):
    b, g, p = re.shape
    nb = g // SSM_GROUPS_PER_BLOCK
    return jnp.concatenate([re.reshape(b, nb, -1), im.reshape(b, nb, -1)], axis=2).reshape(b, -1)


def _from_state_layout(h, g, p):
    b = h.shape[0]
    nb = g // SSM_GROUPS_PER_BLOCK
    h4 = h.reshape(b, nb, 2, SSM_GROUPS_PER_BLOCK * p)
    return h4[:, :, 0].reshape(b, g, p), h4[:, :, 1].reshape(b, g, p)


def _s5_prompt(u, p, batch, seq, g, n_state):
    w = u.shape[1]
    lc = SSM_CHUNK
    assert seq % lc == 0
    n_chunks = seq // lc
    rows = batch * n_chunks
    u_rows = u.reshape(rows, lc * w)
    zeros = jnp.zeros((rows, p["a"].shape[1]), F32)
    (s_loc,) = _s5_scan(u_rows, zeros, p, lc, emit_y=False)
    hprev, hend = _s5_carry(s_loc, p["a"], batch, n_chunks, lc)
    y, _ = _s5_scan(u_rows, hprev, p, lc, emit_y=True)
    hr, hi = _from_state_layout(hend, g, n_state)
    return y.reshape(batch * seq, w), hr, hi


def _s5_sample(u, h0_re, h0_im, p):
    g, n_state = h0_re.shape[1:]
    y, hend = _s5_scan(u, _to_state_layout(h0_re.astype(F32), h0_im.astype(F32)), p, 1, emit_y=True)
    hr, hi = _from_state_layout(hend, g, n_state)
    return y, hr, hi


def _mix_kernel(x_ref, attn_ref, ssm_ref, ga_ref, gb_ref, wa_ref, ws_ref, wo_ref, g_ref, b_ref, o_ref, *, alpha):
    a = _dot(attn_ref[...], wa_ref[...])
    s = _dot(ssm_ref[...], ws_ref[...])
    merged = jax.nn.sigmoid(ga_ref[...]) * a + jax.nn.sigmoid(gb_ref[...]) * s
    mix = _dot(merged.astype(BF16), wo_ref[...])
    o_ref[...] = _layer_norm(alpha * x_ref[...] + mix, g_ref[...], b_ref[...])


def _mix(x, attn, ssm, ga, gb, wa, ws, wo, g, b, alpha):
    n, d = x.shape
    tm = _row_tile(n, TOKEN_TILE)
    row = lambda w: pl.BlockSpec((tm, w), lambda i: (i, 0))
    return pl.pallas_call(
        functools.partial(_mix_kernel, alpha=alpha),
        grid=(n // tm,),
        in_specs=[row(d), row(attn.shape[1]), row(ssm.shape[1]), row(d), row(d),
                  _resident(wa.shape), _resident(ws.shape), _resident(wo.shape),
                  _resident(g.shape), _resident(b.shape)],
        out_specs=row(d),
        out_shape=jax.ShapeDtypeStruct((n, d), F32),
        compiler_params=_cparams("parallel"),
        name="mix_out_ln",
    )(x, attn, ssm, ga, gb, wa, ws, wo, g, b)


def _ffn_kernel(x_ref, w1_ref, w3_ref, w2_ref, g_ref, b_ref, o_ref, *, alpha, f_chunk):
    x = x_ref[...]
    xb = x.astype(BF16)
    acc = jnp.zeros(x.shape, F32)
    for c in range(w1_ref.shape[1] // f_chunk):
        sl = slice(c * f_chunk, (c + 1) * f_chunk)
        h = jax.nn.silu(_dot(xb, w1_ref[:, sl])) * _dot(xb, w3_ref[:, sl])
        acc = acc + _dot(h.astype(BF16), w2_ref[sl, :])
    o_ref[...] = _layer_norm(alpha * x + acc, g_ref[...], b_ref[...])


def _ffn_chunk(d_ff):
    best = LANES
    for c in range(LANES, d_ff + 1, LANES):
        if d_ff % c == 0 and c <= 1536:
            best = c
    return best


def _ffn(x, w1, w3, w2, g, b, alpha):
    n, d = x.shape
    tm = _row_tile(n, TOKEN_TILE)
    row = pl.BlockSpec((tm, d), lambda i: (i, 0))
    return pl.pallas_call(
        functools.partial(_ffn_kernel, alpha=alpha, f_chunk=_ffn_chunk(w1.shape[1])),
        grid=(n // tm,),
        in_specs=[row, _resident(w1.shape), _resident(w3.shape), _resident(w2.shape),
                  _resident(g.shape), _resident(b.shape)],
        out_specs=row,
        out_shape=jax.ShapeDtypeStruct((n, d), F32),
        compiler_params=_cparams("parallel"),
        name="ffn_ln",
    )(x, w1, w3, w2, g, b)


def _moe_kernel(x_ref, wr_ref, br_ref, w1_ref, w3_ref, w2_ref, g_ref, b_ref, o_ref, comb_ref, acc_ref, *, alpha):
    e = pl.program_id(1)
    x = x_ref[...]
    xb = x.astype(BF16)
    n_exp = comb_ref.shape[1]
    lane = lax.broadcasted_iota(I32, comb_ref.shape, 1).astype(F32)

    @pl.when(e == 0)
    def _():
        logits = _dot(xb, wr_ref[...]) + br_ref[...]
        m1 = jnp.max(logits, axis=1, keepdims=True)
        i1 = jnp.min(jnp.where(logits == m1, lane, float(n_exp)), axis=1, keepdims=True)
        rest = jnp.where(lane == i1, -jnp.inf, logits)
        m2 = jnp.max(rest, axis=1, keepdims=True)
        i2 = jnp.min(jnp.where(rest == m2, lane, float(n_exp)), axis=1, keepdims=True)
        e2 = jnp.exp(m2 - m1)
        den = 1.0 + e2
        comb_ref[...] = jnp.where(lane == i1, 1.0 / den, 0.0) + jnp.where(lane == i2, e2 / den, 0.0)
        acc_ref[...] = jnp.zeros(acc_ref.shape, F32)

    h = jax.nn.silu(_dot(xb, w1_ref[...])) * _dot(xb, w3_ref[...])
    y = _dot(h.astype(BF16), w2_ref[...])
    ce = jnp.sum(jnp.where(lane == e.astype(F32), comb_ref[...], 0.0), axis=1, keepdims=True)
    acc_ref[...] = acc_ref[...] + ce * y

    @pl.when(e == n_exp - 1)
    def _():
        o_ref[...] = _layer_norm(alpha * x + acc_ref[...], g_ref[...], b_ref[...])


def _moe(x, wr, br, w1, w3, w2, g, b, alpha):
    n, d = x.shape
    n_exp, _, f = w1.shape
    tm = _row_tile(n, 2 * TOKEN_TILE)
    row = pl.BlockSpec((tm, d), lambda i, e: (i, 0))
    return pl.pallas_call(
        functools.partial(_moe_kernel, alpha=alpha),
        grid=(n // tm, n_exp),
        in_specs=[row, _resident(wr.shape), _resident(br.shape),
                  pl.BlockSpec((None, d, f), lambda i, e: (e, 0, 0)),
                  pl.BlockSpec((None, d, f), lambda i, e: (e, 0, 0)),
                  pl.BlockSpec((None, f, d), lambda i, e: (e, 0, 0)),
                  _resident(g.shape), _resident(b.shape)],
        out_specs=row,
        out_shape=jax.ShapeDtypeStruct((n, d), F32),
        scratch_shapes=[pltpu.VMEM((tm, n_exp), F32), pltpu.VMEM((tm, d), F32)],
        compiler_params=_cparams("parallel", "arbitrary"),
        name="moe_ln",
    )(x, wr, br, w1, w3, w2, g, b)


def kernel(x_prompt, x_sample, cache_k, cache_v, cache_kidx, state_ssm_re, state_ssm_im, page_table, ln1_g, ln1_b, w_in, w_attn_proj, w_ssm_proj, w_out, ssm_a_re, ssm_a_im, ssm_log_dt, ssm_b_re, ssm_b_im, ssm_c_re, ssm_c_im, ssm_d, ssm_w_glu, ssm_b_glu, ln2_g, ln2_b, ffn_w1, ffn_w3, ffn_w2, moe_w_router, moe_b_router, moe_w1, moe_w3, moe_w2):
    batch, seq, d_model = x_prompt.shape
    dec_batch, dec_seq, _ = x_sample.shape
    assert dec_seq == 1
    depth = w_in.shape[0]
    g_ssm, n_state = ssm_a_re.shape[1:]
    ssm_width = ssm_d.shape[1]
    alpha = (2.0 * depth) ** 0.25
    in_sizes = (ATTN_WIDTH, KV_WIDTH, KV_WIDTH, N_IDX_HEADS * IDX_DIM, IDX_DIM, N_IDX_HEADS,
                ssm_width, d_model, d_model)
    assert sum(in_sizes) == w_in.shape[2]
    splits = np.cumsum(np.array(in_sizes))[:-1].tolist()

    xp = x_prompt.reshape(batch * seq, d_model).astype(F32)
    xs = x_sample.reshape(dec_batch, d_model).astype(F32)
    row2 = lambda v: v.astype(F32).reshape(1, -1)

    outs = [[] for _ in range(10)]
    for l in range(depth):
        w_pieces = [w.astype(BF16) for w in jnp.split(w_in[l], splits, axis=1)]
        wa, ws, wo = w_attn_proj[l].astype(BF16), w_ssm_proj[l].astype(BF16), w_out[l].astype(BF16)
        sp = _s5_params(ssm_a_re[l], ssm_a_im[l], ssm_log_dt[l], ssm_b_re[l], ssm_b_im[l],
                        ssm_c_re[l], ssm_c_im[l], ssm_d[l], ssm_w_glu[l], ssm_b_glu[l])
        g1, b1, g2, b2 = row2(ln1_g[l]), row2(ln1_b[l]), row2(ln2_g[l]), row2(ln2_b[l])

        q, k, v, kb, vb, qi, ki, kib, wi, u, ga, gb = _in_proj(xp, w_pieces)
        attn = _prompt_attention(q, qi, wi, kb, vb, kib, batch, seq)
        ssm, hr_p, hi_p = _s5_prompt(u, sp, batch, seq, g_ssm, n_state)
        xp = _mix(xp, attn, ssm, ga, gb, wa, ws, wo, g1, b1, alpha)
        q_s, k_s, v_s, _, _, qi_s, ki_s, _, wi_s, u_s, ga_s, gb_s = _in_proj(xs, w_pieces)
        attn_s = _sample_attention(q_s, qi_s, wi_s, k_s, v_s, ki_s, cache_k, cache_v, cache_kidx,
                                   page_table, l)
        ssm_s, hr_s, hi_s = _s5_sample(u_s, state_ssm_re[l], state_ssm_im[l], sp)
        xs = _mix(xs, attn_s, ssm_s, ga_s, gb_s, wa, ws, wo, g1, b1, alpha)
        j = l // 2
        if l % 2 == 0:
            w1, w3, w2 = ffn_w1[j].astype(BF16), ffn_w3[j].astype(BF16), ffn_w2[j].astype(BF16)
            xp = _ffn(xp, w1, w3, w2, g2, b2, alpha)
            xs = _ffn(xs, w1, w3, w2, g2, b2, alpha)
        else:
            wr, br = moe_w_router[j].astype(BF16), row2(moe_b_router[j])
            w1, w3, w2 = moe_w1[j].astype(BF16), moe_w3[j].astype(BF16), moe_w2[j].astype(BF16)
            xp = _moe(xp, wr, br, w1, w3, w2, g2, b2, alpha)
            xs = _moe(xs, wr, br, w1, w3, w2, g2, b2, alpha)

        kv_shape = (batch, seq, N_KV_HEADS, HEAD_DIM)
        kvs_shape = (dec_batch, dec_seq, N_KV_HEADS, HEAD_DIM)
        for lst, val in zip(outs, (k.reshape(kv_shape), v.reshape(kv_shape), ki.reshape(batch, seq, IDX_DIM),
                                   hr_p, hi_p, k_s.reshape(kvs_shape), v_s.reshape(kvs_shape),
                                   ki_s.reshape(dec_batch, dec_seq, IDX_DIM), hr_s, hi_s)):
            lst.append(val)

    return (xp.reshape(batch, seq, d_model), xs.reshape(dec_batch, dec_seq, d_model),
            *[jnp.stack(o) for o in outs])
```

```python
import functools
import math

import jax
import jax.numpy as jnp
import numpy as np
from jax import lax
from jax.experimental import pallas as pl
from jax.experimental.pallas import tpu as pltpu

F32 = jnp.float32
BF16 = jnp.bfloat16
I32 = jnp.int32

N_HEADS = 8
HEAD_DIM = 128
N_KV_HEADS = 4
KV_GROUP = N_HEADS // N_KV_HEADS
ATTN_WIDTH = N_HEADS * HEAD_DIM
KV_WIDTH = N_KV_HEADS * HEAD_DIM
N_IDX_HEADS = 8
IDX_DIM = 64
TOPK_MAX = 256
SSM_GROUP = 16
SSM_STATE = 64
TOP_K_EXPERTS = 2
LN_EPS = 1e-5
NEG_BIG = -1e30

LANES = 128
SUBLANES = 8
VMEM_LIMIT_BYTES = 56 * 1024 * 1024

TOKEN_TILE = 256
Q_TILE = 256
ATTN_ROW_BLOCK = 128
KEY_CHUNK = 512
COUNT_ROW_BLOCK = 128
SSM_CHUNK = 16
SSM_ROW_TILE = 128
SSM_GROUPS_PER_BLOCK = LANES // SSM_GROUP
SSM_BLOCK_STATES = SSM_GROUPS_PER_BLOCK * SSM_STATE
PAGES_PER_STEP = 16
MOE_TOKEN_TILE = 512
MOE_EXPERT_ROWS = 128

INT_MIN = -(2 ** 31)
KEY_NEG_INF = int(np.int32(np.uint32(0xFF800000)) ^ np.int32(0x7FFFFFFF))


def _cparams(*sem):
    return pltpu.CompilerParams(dimension_semantics=sem, vmem_limit_bytes=VMEM_LIMIT_BYTES)


def _resident(shape):
    nd = len(shape)
    return pl.BlockSpec(tuple(shape), lambda *_: (0,) * nd, pipeline_mode=pl.Buffered(1))


def _row_tile(n, want):
    t = min(n, want)
    assert n % t == 0, (n, t)
    return t


def _layer_norm(x, g, b):
    mu = jnp.mean(x, axis=-1, keepdims=True)
    xc = x - mu
    var = jnp.mean(xc * xc, axis=-1, keepdims=True)
    return xc * lax.rsqrt(var + LN_EPS) * g + b


def _dot(a, b):
    return jnp.dot(a, b, preferred_element_type=F32)


def _dot_nt(a, b):
    return lax.dot_general(a, b, (((1,), (1,)), ((), ())), preferred_element_type=F32)


def _in_proj_kernel(x_ref, wq, wk, wv, wqi, wki, wwi, wu, wga, wgb,
                    q_o, k_o, v_o, kb_o, vb_o, qi_o, ki_o, kib_o, wi_o, u_o, ga_o, gb_o):
    xb = x_ref[...].astype(BF16)
    q = _dot(xb, wq[...]).astype(BF16)
    for h in range(N_HEADS):
        q_o[h] = q[:, h * HEAD_DIM:(h + 1) * HEAD_DIM]
    k = _dot(xb, wk[...])
    k_o[...] = k
    kb_o[...] = k.astype(BF16)
    v = _dot(xb, wv[...])
    v_o[...] = v
    vb = v.astype(BF16)
    for g in range(N_KV_HEADS):
        vb_o[g] = vb[:, g * HEAD_DIM:(g + 1) * HEAD_DIM]
    qi = _dot(xb, wqi[...]).astype(BF16)
    for h in range(N_IDX_HEADS):
        qi_o[h] = qi[:, h * IDX_DIM:(h + 1) * IDX_DIM]
    ki = _dot(xb, wki[...])
    ki_o[...] = ki
    kib_o[...] = ki.astype(BF16)
    wi_o[...] = _dot(xb, wwi[...])
    u_o[...] = _dot(xb, wu[...])
    ga_o[...] = _dot(xb, wga[...])
    gb_o[...] = _dot(xb, wgb[...])


def _in_proj(x, ws):
    n, d = x.shape
    tm = _row_tile(n, TOKEN_TILE)
    widths = [w.shape[1] for w in ws]
    aw, kvw, _, qiw, kiw, wiw, uw, gw, _ = widths
    row = lambda w: pl.BlockSpec((tm, w), lambda i: (i, 0))
    out_shape = [
        jax.ShapeDtypeStruct((N_HEADS, n, HEAD_DIM), BF16),
        jax.ShapeDtypeStruct((n, kvw), F32),
        jax.ShapeDtypeStruct((n, kvw), F32),
        jax.ShapeDtypeStruct((n, kvw), BF16),
        jax.ShapeDtypeStruct((N_KV_HEADS, n, HEAD_DIM), BF16),
        jax.ShapeDtypeStruct((N_IDX_HEADS, n, IDX_DIM), BF16),
        jax.ShapeDtypeStruct((n, kiw), F32),
        jax.ShapeDtypeStruct((n, kiw), BF16),
        jax.ShapeDtypeStruct((n, wiw), F32),
        jax.ShapeDtypeStruct((n, uw), F32),
        jax.ShapeDtypeStruct((n, gw), F32),
        jax.ShapeDtypeStruct((n, gw), F32),
    ]
    out_specs = [pl.BlockSpec((N_HEADS, tm, HEAD_DIM), lambda i: (0, i, 0)),
                 row(kvw), row(kvw), row(kvw),
                 pl.BlockSpec((N_KV_HEADS, tm, HEAD_DIM), lambda i: (0, i, 0)),
                 pl.BlockSpec((N_IDX_HEADS, tm, IDX_DIM), lambda i: (0, i, 0)),
                 row(kiw), row(kiw), row(wiw), row(uw), row(gw), row(gw)]
    return pl.pallas_call(
        _in_proj_kernel,
        grid=(n // tm,),
        in_specs=[row(d)] + [_resident(w.shape) for w in ws],
        out_specs=out_specs,
        out_shape=out_shape,
        compiler_params=_cparams("parallel"),
        name="in_proj",
    )(x, *ws)


def _score_to_key(score):
    score = jnp.where(score == 0.0, 0.0, score)
    bits = lax.bitcast_convert_type(score, I32)
    return bits ^ ((bits >> 31) & 0x7FFFFFFF)


def _select_topk(keys_ref, rows, nkc, topk, idx_bits):
    ck = KEY_CHUNK
    rb = min(rows, COUNT_ROW_BLOCK)
    assert rows % rb == 0
    row_blocks = [slice(r0, r0 + rb) for r0 in range(0, rows, rb)]
    lane = lax.broadcasted_iota(I32, (rb, ck), 1)

    def count(pred):
        parts = []
        for rs in row_blocks:
            def body(c, acc, rs=rs):
                off = pl.multiple_of(c * ck, ck)
                m = jnp.where(pred(keys_ref[rs, pl.ds(off, ck)], off, rs), 1.0, 0.0)
                part = m[:, 0:LANES]
                for t in range(1, ck // LANES):
                    part = part + m[:, t * LANES:(t + 1) * LANES]
                return acc + part
            parts.append(lax.fori_loop(0, nkc, body, jnp.zeros((rb, LANES), F32)))
        acc = parts[0] if len(parts) == 1 else jnp.concatenate(parts, axis=0)
        return jnp.sum(acc, axis=1, keepdims=True)

    kf = float(topk)
    t0 = jnp.where(count(lambda kc, off, rs: kc >= 0) >= kf, 0, INT_MIN).astype(I32)

    def bit_body(j, t):
        cand = t + jnp.left_shift(jnp.int32(1), 30 - j)
        return jnp.where(count(lambda kc, off, rs: kc >= cand[rs]) >= kf, cand, t)

    t = lax.fori_loop(0, 31, bit_body, t0)
    t = jnp.maximum(t, KEY_NEG_INF + 1)

    cnt_gt = count(lambda kc, off, rs: kc > t[rs])
    cnt_ge = count(lambda kc, off, rs: kc >= t[rs])
    need = kf - cnt_gt
    tie = (cnt_ge - cnt_gt) > need

    @pl.when(jnp.max(jnp.where(tie, 1.0, 0.0)) > 0.0)
    def _():
        def idx_body(j, m):
            cand = m + jnp.left_shift(jnp.int32(1), idx_bits - 1 - j)
            c = count(lambda kc, off, rs: jnp.where(kc == t[rs], off + lane, cand[rs]) < cand[rs])
            return jnp.where(c < need, cand, m)

        m = lax.fori_loop(0, idx_bits, idx_body, jnp.zeros((rows, 1), I32))
        last = jnp.where(tie, m, 2 ** 30)

        for rs in row_blocks:
            def fix(c, carry, rs=rs):
                off = pl.multiple_of(c * ck, ck)
                kc = keys_ref[rs, pl.ds(off, ck)]
                lose = jnp.where(kc == t[rs], off + lane, -1) > last[rs]
                keys_ref[rs, pl.ds(off, ck)] = jnp.where(lose, kc - 1, kc)
                return carry

            lax.fori_loop(0, nkc, fix, 0)

    return t


def _prompt_attn_kernel(q_ref, qi_ref, wi_ref, kt_ref, v_ref, kit_ref, o_ref,
                        keys_ref, m_ref, l_ref, acc_ref, alpha_ref, s_ref, p_ref, *, topk, idx_bits):
    tq, ck = Q_TILE, KEY_CHUNK
    i = pl.program_id(1)
    row0 = i * tq
    nkc = (row0 + tq + ck - 1) // ck
    row = row0 + lax.broadcasted_iota(I32, (tq, ck), 0)
    lane = lax.broadcasted_iota(I32, (tq, ck), 1)
    wi = wi_ref[...]

    def score_body(c, carry):
        off = pl.multiple_of(c * ck, ck)
        kic = kit_ref[:, pl.ds(off, ck)]
        acc = jnp.zeros((tq, ck), F32)
        for h in range(N_IDX_HEADS):
            d = _dot(qi_ref[h], kic)
            acc = acc + wi[:, h:h + 1] * jnp.maximum(d, 0.0)
        acc = jnp.where(off + lane <= row, acc, -jnp.inf)
        keys_ref[:, pl.ds(off, ck)] = _score_to_key(acc)
        return carry

    lax.fori_loop(0, nkc, score_body, 0)
    t = _select_topk(keys_ref, tq, nkc, topk, idx_bits)

    c2 = (HEAD_DIM ** -0.5) * math.log2(math.e)
    rb = ATTN_ROW_BLOCK

    for r0 in range(0, tq, rb):
        rs = slice(r0, r0 + rb)
        n_units = ((row0 + r0 + rb + ck - 1) // ck) * N_KV_HEADS
        t_b = t[rs]
        m_ref[...] = jnp.full(m_ref.shape, NEG_BIG, F32)
        l_ref[...] = jnp.zeros(l_ref.shape, F32)
        acc_ref[...] = jnp.zeros(acc_ref.shape, F32)
        s_ref[...] = jnp.full(s_ref.shape, NEG_BIG, F32)
        p_ref[...] = jnp.zeros(p_ref.shape, BF16)
        alpha_ref[...] = jnp.ones(alpha_ref.shape, F32)

        def attn_step(n, carry, rs=rs, n_units=n_units, t_b=t_b):
            off_c = pl.multiple_of((jnp.maximum(n - 2, 0) // N_KV_HEADS) * ck, ck)
            g_c = (n + 2) % N_KV_HEADS
            slot_c = n % 2
            vc = v_ref[g_c, pl.ds(off_c, ck), :]
            for j in range(KV_GROUP):
                h = g_c * KV_GROUP + j
                acc_ref[h] = alpha_ref[slot_c, j] * acc_ref[h] + _dot(p_ref[slot_c, j], vc)
            g_b = (n + 3) % N_KV_HEADS
            slot_b = (n + 1) % 2
            for j in range(KV_GROUP):
                h = g_b * KV_GROUP + j
                s = s_ref[slot_b, j]
                m_old = m_ref[h]
                m_new = jnp.maximum(m_old, jnp.max(s, axis=1, keepdims=True))
                p = jnp.exp2((s - m_new) * c2)
                alpha = jnp.exp2((m_old - m_new) * c2)
                l_ref[h] = alpha * l_ref[h] + jnp.sum(p, axis=1, keepdims=True)
                p_ref[slot_b, j] = p.astype(BF16)
                alpha_ref[slot_b, j] = alpha
                m_ref[h] = m_new
            unit_a = jnp.minimum(n, n_units - 1)
            off_a = pl.multiple_of((unit_a // N_KV_HEADS) * ck, ck)
            g_a = unit_a % N_KV_HEADS
            t_n = jnp.where(n < n_units, t_b, 2 ** 31 - 1)
            bias = jnp.where(keys_ref[rs, pl.ds(off_a, ck)] >= t_n, 0.0, NEG_BIG)
            kc = kt_ref[g_a, :, pl.ds(off_a, ck)]
            for j in range(KV_GROUP):
                s_ref[slot_c, j] = _dot(q_ref[g_a * KV_GROUP + j, rs, :], kc) + bias
            return carry

        lax.fori_loop(0, n_units + 2, attn_step, 0)
        for h in range(N_HEADS):
            o_ref[rs, h * HEAD_DIM:(h + 1) * HEAD_DIM] = (acc_ref[h] / l_ref[h]).astype(o_ref.dtype)


def _prompt_attention(q, qi, wi, kb, vb, kib, batch, seq):
    topk = min(TOPK_MAX, seq // 4)
    assert seq % KEY_CHUNK == 0 and seq % Q_TILE == 0
    idx_bits = max(1, (seq - 1).bit_length())
    q4 = q.reshape(N_HEADS, batch, seq, HEAD_DIM)
    qi4 = qi.reshape(N_IDX_HEADS, batch, seq, IDX_DIM)
    wi3 = wi.reshape(batch, seq, N_IDX_HEADS)
    kt4 = jnp.swapaxes(kb.reshape(batch, seq, KV_WIDTH), 1, 2).reshape(batch, N_KV_HEADS, HEAD_DIM, seq)
    v4 = vb.reshape(N_KV_HEADS, batch, seq, HEAD_DIM)
    kit3 = jnp.swapaxes(kib.reshape(batch, seq, IDX_DIM), 1, 2)
    tq, rb = Q_TILE, ATTN_ROW_BLOCK
    assert tq % rb == 0
    out = pl.pallas_call(
        functools.partial(_prompt_attn_kernel, topk=topk, idx_bits=idx_bits),
        grid=(batch, seq // tq),
        in_specs=[
            pl.BlockSpec((N_HEADS, None, tq, HEAD_DIM), lambda b, i: (0, b, i, 0)),
            pl.BlockSpec((N_IDX_HEADS, None, tq, IDX_DIM), lambda b, i: (0, b, i, 0)),
            pl.BlockSpec((None, tq, N_IDX_HEADS), lambda b, i: (b, i, 0)),
            pl.BlockSpec((None, N_KV_HEADS, HEAD_DIM, seq), lambda b, i: (b, 0, 0, 0)),
            pl.BlockSpec((N_KV_HEADS, None, seq, HEAD_DIM), lambda b, i: (0, b, 0, 0)),
            pl.BlockSpec((None, IDX_DIM, seq), lambda b, i: (b, 0, 0)),
        ],
        out_specs=pl.BlockSpec((None, tq, ATTN_WIDTH), lambda b, i: (b, i, 0)),
        out_shape=jax.ShapeDtypeStruct((batch, seq, ATTN_WIDTH), BF16),
        scratch_shapes=[pltpu.VMEM((tq, seq), I32),
                        pltpu.VMEM((N_HEADS, rb, 1), F32),
                        pltpu.VMEM((N_HEADS, rb, 1), F32),
                        pltpu.VMEM((N_HEADS, rb, HEAD_DIM), F32),
                        pltpu.VMEM((2, KV_GROUP, rb, 1), F32),
                        pltpu.VMEM((2, KV_GROUP, rb, KEY_CHUNK), F32),
                        pltpu.VMEM((2, KV_GROUP, rb, KEY_CHUNK), BF16)],
        compiler_params=_cparams("parallel", "arbitrary"),
        name="prompt_attention",
    )(q4, qi4, wi3, kt4, v4, kit3)
    return out.reshape(batch * seq, ATTN_WIDTH)


def _sample_scores_kernel(pt_ref, qi_ref, wi_ref, kin_ref, *rest, n_groups):
    page_refs, keys_ref = rest[:PAGES_PER_STEP], rest[PAGES_PER_STEP]
    j = pl.program_id(1)
    qi = qi_ref[...]
    wi = wi_ref[...]

    @pl.when(j < n_groups)
    def _():
        for p in range(PAGES_PER_STEP):
            d = _dot(qi, page_refs[p][...].astype(BF16))
            s = jnp.sum(wi * jnp.maximum(d, 0.0), axis=0, keepdims=True)
            keys_ref[p:p + 1, :] = _score_to_key(s)

    @pl.when(j == n_groups)
    def _():
        kin = kin_ref[...].astype(BF16).astype(F32)
        d = jnp.sum(qi.astype(F32) * kin, axis=1, keepdims=True)
        s = jnp.sum(wi * jnp.maximum(d, 0.0), axis=0, keepdims=True)
        first = (lax.broadcasted_iota(I32, keys_ref.shape, 0) == 0) & (
            lax.broadcasted_iota(I32, keys_ref.shape, 1) == 0)
        keys_ref[...] = jnp.where(first, _score_to_key(s), KEY_NEG_INF)


def _sample_select_kernel(keys_in_ref, tri_ref, digits_ref, pos_ref, cnt_ref, keys_ref, rank_ref,
                          *, topk, idx_bits, past):
    rows, width = keys_in_ref.shape
    ck = KEY_CHUNK
    keys_ref[...] = keys_in_ref[...]
    t = _select_topk(keys_ref, rows, width // ck, topk, idx_bits)

    rank_ref[...] = jnp.full(rank_ref.shape, -1.0, F32)
    carry = jnp.zeros((rows, 1), F32)
    for c in range(past // LANES):
        sl = slice(c * LANES, (c + 1) * LANES)
        m = jnp.where(keys_ref[:, sl] >= t, 1.0, 0.0)
        incl = _dot(m.astype(BF16), tri_ref[...])
        rank_ref[:, sl] = jnp.where(m > 0.0, carry + incl - 1.0, -1.0)
        carry = carry + incl[:, LANES - 1:LANES]
    cnt_ref[...] = carry

    want = lax.broadcasted_iota(I32, (topk, 1), 0).astype(F32)

    def row_body(b, carry_):
        acc = jnp.zeros((topk, LANES), F32)
        for c in range(past // ck):
            r = rank_ref[pl.ds(b, 1), c * ck:(c + 1) * ck]
            onehot = jnp.where(r == want, 1.0, 0.0).astype(BF16)
            acc = acc + _dot(onehot, digits_ref[c * ck:(c + 1) * ck, :])
        pos_ref[b] = acc
        return carry_

    lax.fori_loop(0, rows, row_body, 0)


def _sample_attend_kernel(pos_ref, cnt_ref, pt_ref, q_ref, kn_ref, vn_ref, ck_hbm, cv_hbm, o_ref,
                          kbuf, vbuf, sem, *, layer, n_pages, page, topk):
    b = pl.program_id(0)
    kvh = N_KV_HEADS

    def row_copies(bb, i, slot):
        pos = pos_ref[bb * topk + i]
        pg = pt_ref[bb * n_pages + pos // page]
        src = pl.ds(pl.multiple_of((pos % page) * kvh, kvh), kvh)
        dst = pl.ds(pl.multiple_of(i * kvh, kvh), kvh)
        return (pltpu.make_async_copy(ck_hbm.at[layer, pg, src, :], kbuf.at[slot, dst, :], sem.at[0, slot]),
                pltpu.make_async_copy(cv_hbm.at[layer, pg, src, :], vbuf.at[slot, dst, :], sem.at[1, slot]))

    def start_all(bb, slot):
        def body(i, carry):
            for cp in row_copies(bb, i, slot):
                cp.start()
            return carry
        lax.fori_loop(0, topk, body, 0)

    @pl.when(b == 0)
    def _():
        start_all(0, 0)

    @pl.when(b + 1 < pl.num_programs(0))
    def _():
        start_all(b + 1, (b + 1) % 2)

    slot = b % 2

    def wait_body(i, carry):
        for cp in row_copies(b, i, slot):
            cp.wait()
        return carry

    lax.fori_loop(0, topk, wait_body, 0)

    scale = HEAD_DIM ** -0.5
    q = q_ref[...]
    n_rows = topk * kvh
    col = lax.broadcasted_iota(I32, (N_HEADS, n_rows), 1)
    head_group = lax.broadcasted_iota(I32, (N_HEADS, n_rows), 0) // KV_GROUP
    cnt = cnt_ref[b]
    valid = (col % kvh == head_group) & (col // kvh < cnt)
    s = jnp.where(valid, _dot_nt(q, kbuf[slot].astype(BF16)) * scale, NEG_BIG)

    kn = kn_ref[...].astype(BF16).astype(F32)
    vn = vn_ref[...].astype(BF16).astype(F32)
    qf = q.astype(F32)
    row_group = lax.broadcasted_iota(I32, (N_HEADS, 1), 0) // KV_GROUP
    s_new = jnp.zeros((N_HEADS, 1), F32)
    v_new = jnp.zeros((N_HEADS, HEAD_DIM), F32)
    for g in range(kvh):
        sl = slice(g * HEAD_DIM, (g + 1) * HEAD_DIM)
        s_new = jnp.where(row_group == g, jnp.sum(qf * kn[:, sl], axis=1, keepdims=True), s_new)
        v_new = jnp.where(row_group == g, vn[:, sl], v_new)
    new_sel = cnt < topk
    s_new = jnp.where(new_sel, s_new * scale, NEG_BIG)

    m = jnp.maximum(jnp.max(s, axis=1, keepdims=True), s_new)
    p = jnp.where(valid, jnp.exp(s - m), 0.0)
    p_new = jnp.where(new_sel, jnp.exp(s_new - m), 0.0)
    l = jnp.sum(p, axis=1, keepdims=True) + p_new
    acc = _dot(p.astype(BF16), vbuf[slot].astype(BF16)) + p_new.astype(BF16).astype(F32) * v_new
    o_ref[...] = (acc / l).astype(o_ref.dtype)


def _sample_attention(q, qi, wi, k_new, v_new, ki_new, cache_k, cache_v, cache_kidx, page_table, layer):
    db, n_pages = page_table.shape
    page = cache_k.shape[2]
    npg = PAGES_PER_STEP
    assert n_pages % npg == 0 and page == LANES
    n_groups = n_pages // npg
    past = n_pages * page
    topk = min(TOPK_MAX, (past + 1) // 4)
    n_rows = n_pages + npg
    width = n_rows * page
    assert width % KEY_CHUNK == 0
    idx_bits = (width - 1).bit_length()
    pt = page_table.reshape(-1).astype(I32)

    page_rows = page * N_KV_HEADS
    ck4 = cache_k.reshape(cache_k.shape[0], cache_k.shape[1], page_rows, HEAD_DIM)
    cv4 = cache_v.reshape(cache_v.shape[0], cache_v.shape[1], page_rows, HEAD_DIM)
    kidx_t = jnp.swapaxes(cache_kidx, 2, 3)
    qi3 = jnp.transpose(qi, (1, 0, 2))
    wi3 = wi.reshape(db, N_IDX_HEADS, 1)
    q3 = jnp.transpose(q, (1, 0, 2))

    def page_spec(rows_, width_, p):
        def imap(b, j, pt_ref):
            pg = jnp.minimum(j * npg + p, n_pages - 1)
            return (layer, pt_ref[b * n_pages + pg], 0, 0)
        return pl.BlockSpec((None, None, rows_, width_), imap)

    keys = pl.pallas_call(
        functools.partial(_sample_scores_kernel, n_groups=n_groups),
        grid_spec=pltpu.PrefetchScalarGridSpec(
            num_scalar_prefetch=1,
            grid=(db, n_groups + 1),
            in_specs=[
                pl.BlockSpec((None, N_IDX_HEADS, IDX_DIM), lambda b, j, pt_ref: (b, 0, 0)),
                pl.BlockSpec((None, N_IDX_HEADS, 1), lambda b, j, pt_ref: (b, 0, 0)),
                pl.BlockSpec((None, 1, IDX_DIM), lambda b, j, pt_ref: (b, 0, 0)),
            ] + [page_spec(IDX_DIM, page, p) for p in range(npg)],
            out_specs=pl.BlockSpec((None, npg, page), lambda b, j, pt_ref: (b, j, 0)),
        ),
        out_shape=jax.ShapeDtypeStruct((db, n_rows, page), I32),
        compiler_params=_cparams("parallel", "arbitrary"),
        name="sample_scores",
    )(pt, qi3, wi3, ki_new.reshape(db, 1, IDX_DIM), *([kidx_t] * npg))

    tri = (jnp.arange(LANES)[:, None] <= jnp.arange(LANES)[None, :]).astype(BF16)
    positions = jnp.arange(past)
    digits = jnp.zeros((past, LANES), BF16).at[:, 0].set((positions // LANES).astype(BF16))
    digits = digits.at[:, 1].set((positions % LANES).astype(BF16))
    pos_digits, cnt = pl.pallas_call(
        functools.partial(_sample_select_kernel, topk=topk, idx_bits=idx_bits, past=past),
        out_shape=[jax.ShapeDtypeStruct((db, topk, LANES), F32), jax.ShapeDtypeStruct((db, 1), F32)],
        scratch_shapes=[pltpu.VMEM((db, width), I32), pltpu.VMEM((db, width), F32)],
        compiler_params=pltpu.CompilerParams(vmem_limit_bytes=VMEM_LIMIT_BYTES),
        name="sample_select",
    )(keys.reshape(db, width), tri, digits)
    pos = (pos_digits[:, :, 0] * LANES + pos_digits[:, :, 1]).astype(I32).reshape(-1)
    cnt = cnt.astype(I32).reshape(-1)

    out = pl.pallas_call(
        functools.partial(_sample_attend_kernel, layer=layer, n_pages=n_pages, page=page, topk=topk),
        grid_spec=pltpu.PrefetchScalarGridSpec(
            num_scalar_prefetch=3,
            grid=(db,),
            in_specs=[
                pl.BlockSpec((None, N_HEADS, HEAD_DIM), lambda b, *_: (b, 0, 0)),
                pl.BlockSpec((None, 1, KV_WIDTH), lambda b, *_: (b, 0, 0)),
                pl.BlockSpec((None, 1, KV_WIDTH), lambda b, *_: (b, 0, 0)),
                pl.BlockSpec(memory_space=pl.ANY),
                pl.BlockSpec(memory_space=pl.ANY),
            ],
            out_specs=pl.BlockSpec((None, N_HEADS, HEAD_DIM), lambda b, *_: (b, 0, 0)),
            scratch_shapes=[pltpu.VMEM((2, topk * N_KV_HEADS, HEAD_DIM), F32),
                            pltpu.VMEM((2, topk * N_KV_HEADS, HEAD_DIM), F32),
                            pltpu.SemaphoreType.DMA((2, 2))],
        ),
        out_shape=jax.ShapeDtypeStruct((db, N_HEADS, HEAD_DIM), BF16),
        compiler_params=_cparams("arbitrary"),
        name="sample_attend",
    )(pos, cnt, pt, q3, k_new.reshape(db, 1, KV_WIDTH), v_new.reshape(db, 1, KV_WIDTH), ck4, cv4)
    return out.reshape(db, ATTN_WIDTH)


def _s5_scan_kernel(u_ref, h0_ref, a_ref, bbd_ref, cbd_ref, d_ref, wglu_ref, bglu_ref,
                    *out_refs, n_steps, emit_y):
    if emit_y:
        y_ref, hend_ref = out_refs
    else:
        (hend_ref,) = out_refs
    width = SSM_GROUPS_PER_BLOCK * SSM_GROUP * (bbd_ref.shape[0])
    nb = bbd_ref.shape[0]
    bs = SSM_BLOCK_STATES
    h = [None] * nb
    for blk in range(nb):
        h[blk] = (h0_ref[:, blk * 2 * bs:blk * 2 * bs + bs], h0_ref[:, blk * 2 * bs + bs:(blk + 1) * 2 * bs])
    for tau in range(n_steps):
        u = u_ref[:, tau * width:(tau + 1) * width]
        ub = u.astype(BF16)
        ys = []
        for blk in range(nb):
            bu = _dot(ub[:, blk * LANES:(blk + 1) * LANES], bbd_ref[blk])
            ar = a_ref[0:1, blk * 2 * bs:blk * 2 * bs + bs]
            ai = a_ref[0:1, blk * 2 * bs + bs:(blk + 1) * 2 * bs]
            hr, hi = h[blk]
            nr = ar * hr - ai * hi + bu[:, :bs]
            ni = ar * hi + ai * hr + bu[:, bs:]
            h[blk] = (nr, ni)
            if emit_y:
                ys.append(_dot(nr.astype(BF16), cbd_ref[blk, :bs, :])
                          - _dot(ni.astype(BF16), cbd_ref[blk, bs:, :]))
        if emit_y:
            y = jnp.concatenate(ys, axis=1) + d_ref[...] * u
            y = jax.nn.gelu(y)
            z = _dot(y.astype(BF16), wglu_ref[...]) + bglu_ref[...]
            y_ref[:, tau * width:(tau + 1) * width] = (y * jax.nn.sigmoid(z)).astype(y_ref.dtype)
    for blk in range(nb):
        hend_ref[:, blk * 2 * bs:blk * 2 * bs + bs] = h[blk][0]
        hend_ref[:, blk * 2 * bs + bs:(blk + 1) * 2 * bs] = h[blk][1]


def _s5_scan(u_rows, h0, p, n_steps, emit_y):
    r = u_rows.shape[0]
    tr = _row_tile(r, SSM_ROW_TILE)
    hw = h0.shape[1]
    row = lambda w: pl.BlockSpec((tr, w), lambda i: (i, 0))
    out_shape = [jax.ShapeDtypeStruct((r, hw), F32)]
    out_specs = [row(hw)]
    if emit_y:
        out_shape = [jax.ShapeDtypeStruct(u_rows.shape, BF16)] + out_shape
        out_specs = [row(u_rows.shape[1])] + out_specs
    consts = [p["a"], p["bbd"], p["cbd"], p["d"], p["wglu"], p["bglu"]]
    return pl.pallas_call(
        functools.partial(_s5_scan_kernel, n_steps=n_steps, emit_y=emit_y),
        grid=(r // tr,),
        in_specs=[row(u_rows.shape[1]), row(hw)] + [_resident(c.shape) for c in consts],
        out_specs=out_specs,
        out_shape=out_shape,
        compiler_params=_cparams("parallel"),
        name="s5_scan_y" if emit_y else "s5_scan_state",
    )(u_rows, h0, *consts)


def _s5_carry_kernel(s_ref, a_ref, hprev_ref, hend_ref, *, n_chunks, chunk_len):
    hw = s_ref.shape[1]
    piece = SSM_BLOCK_STATES
    for blk in range(hw // (2 * piece)):
        re = pl.ds(blk * 2 * piece, piece)
        im = pl.ds(blk * 2 * piece + piece, piece)
        ar, ai = a_ref[0:1, re], a_ref[0:1, im]
        pr, pi = ar, ai
        for _ in range(chunk_len - 1):
            pr, pi = pr * ar - pi * ai, pr * ai + pi * ar

        def body(c, carry, re=re, im=im, pr=pr, pi=pi):
            hr, hi = carry
            hprev_ref[pl.ds(c, 1), re] = hr
            hprev_ref[pl.ds(c, 1), im] = hi
            sr = s_ref[pl.ds(c, 1), re]
            si = s_ref[pl.ds(c, 1), im]
            return pr * hr - pi * hi + sr, pr * hi + pi * hr + si

        zero = jnp.zeros((1, piece), F32)
        hr, hi = lax.fori_loop(0, n_chunks, body, (zero, zero))
        hend_ref[0:1, re] = hr
        hend_ref[0:1, im] = hi


def _s5_carry(s, a, batch, n_chunks, chunk_len):
    hw = s.shape[1]
    s3 = s.reshape(batch, n_chunks, hw)
    hprev, hend = pl.pallas_call(
        functools.partial(_s5_carry_kernel, n_chunks=n_chunks, chunk_len=chunk_len),
        grid=(batch,),
        in_specs=[pl.BlockSpec((None, n_chunks, hw), lambda b: (b, 0, 0)), _resident(a.shape)],
        out_specs=[pl.BlockSpec((None, n_chunks, hw), lambda b: (b, 0, 0)),
                   pl.BlockSpec((None, 1, hw), lambda b: (b, 0, 0))],
        out_shape=[jax.ShapeDtypeStruct((batch, n_chunks, hw), F32),
                   jax.ShapeDtypeStruct((batch, 1, hw), F32)],
        compiler_params=_cparams("parallel"),
        name="s5_carry",
    )(s3, a)
    return hprev.reshape(batch * n_chunks, hw), hend.reshape(batch, hw)


def _s5_params(a_re, a_im, log_dt, b_re, b_im, c_re, c_im, d_skip, w_glu, b_glu):
    g, p = a_re.shape
    c = b_re.shape[2]
    gpb = SSM_GROUPS_PER_BLOCK
    nb = g // gpb
    lr, li = a_re.astype(F32), a_im.astype(F32)
    dt = jnp.exp(log_dt.astype(F32))[:, None]
    mag = jnp.exp(lr * dt)
    ab_re, ab_im = mag * jnp.cos(li * dt), mag * jnp.sin(li * dt)
    den = lr * lr + li * li
    nr = ab_re - 1.0
    f_re = (nr * lr + ab_im * li) / den
    f_im = (ab_im * lr - nr * li) / den
    br, bi = b_re.astype(F32), b_im.astype(F32)
    bb_re = f_re[..., None] * br - f_im[..., None] * bi
    bb_im = f_re[..., None] * bi + f_im[..., None] * br
    eye = jnp.eye(gpb, dtype=F32)

    def state_layout(re, im):
        return jnp.concatenate([re.reshape(nb, gpb * p), im.reshape(nb, gpb * p)], axis=1).reshape(1, -1)

    def in_block(x):
        x = jnp.transpose(x.reshape(nb, gpb, p, c), (0, 1, 3, 2))
        return (x[:, :, :, None, :] * eye[None, :, None, :, None]).reshape(nb, gpb * c, gpb * p)

    def out_block(x):
        x = jnp.transpose(x.reshape(nb, gpb, c, p), (0, 1, 3, 2))
        return (x[:, :, :, None, :] * eye[None, :, None, :, None]).reshape(nb, gpb * p, gpb * c)

    bbd = jnp.concatenate([in_block(bb_re), in_block(bb_im)], axis=2).astype(BF16)
    cbd = jnp.concatenate([out_block(c_re.astype(F32)), out_block(c_im.astype(F32))], axis=1).astype(BF16)
    return {"a": state_layout(ab_re, ab_im), "bbd": bbd, "cbd": cbd,
            "d": d_skip.astype(F32).reshape(1, -1), "wglu": w_glu.astype(BF16),
            "bglu": b_glu.astype(F32).reshape(1, -1)}


def _to_state_layout(re, im):
    b, g, p = re.shape
    nb = g // SSM_GROUPS_PER_BLOCK
    return jnp.concatenate([re.reshape(b, nb, -1), im.reshape(b, nb, -1)], axis=2).reshape(b, -1)


def _from_state_layout(h, g, p):
    b = h.shape[0]
    nb = g // SSM_GROUPS_PER_BLOCK
    h4 = h.reshape(b, nb, 2, SSM_GROUPS_PER_BLOCK * p)
    return h4[:, :, 0].reshape(b, g, p), h4[:, :, 1].reshape(b, g, p)


def _s5_prompt(u, p, batch, seq, g, n_state):
    w = u.shape[1]
    lc = SSM_CHUNK
    assert seq % lc == 0
    n_chunks = seq // lc
    rows = batch * n_chunks
    u_rows = u.reshape(rows, lc * w)
    zeros = jnp.zeros((rows, p["a"].shape[1]), F32)
    (s_loc,) = _s5_scan(u_rows, zeros, p, lc, emit_y=False)
    hprev, hend = _s5_carry(s_loc, p["a"], batch, n_chunks, lc)
    y, _ = _s5_scan(u_rows, hprev, p, lc, emit_y=True)
    hr, hi = _from_state_layout(hend, g, n_state)
    return y.reshape(batch * seq, w), hr, hi


def _s5_sample(u, h0_re, h0_im, p):
    g, n_state = h0_re.shape[1:]
    y, hend = _s5_scan(u, _to_state_layout(h0_re.astype(F32), h0_im.astype(F32)), p, 1, emit_y=True)
    hr, hi = _from_state_layout(hend, g, n_state)
    return y, hr, hi


def _mix_kernel(x_ref, attn_ref, ssm_ref, ga_ref, gb_ref, wa_ref, ws_ref, wo_ref, g_ref, b_ref, o_ref, *, alpha):
    a = _dot(attn_ref[...], wa_ref[...])
    s = _dot(ssm_ref[...], ws_ref[...])
    merged = jax.nn.sigmoid(ga_ref[...]) * a + jax.nn.sigmoid(gb_ref[...]) * s
    mix = _dot(merged.astype(BF16), wo_ref[...])
    o_ref[...] = _layer_norm(alpha * x_ref[...] + mix, g_ref[...], b_ref[...])


def _mix(x, attn, ssm, ga, gb, wa, ws, wo, g, b, alpha):
    n, d = x.shape
    tm = _row_tile(n, TOKEN_TILE)
    row = lambda w: pl.BlockSpec((tm, w), lambda i: (i, 0))
    return pl.pallas_call(
        functools.partial(_mix_kernel, alpha=alpha),
        grid=(n // tm,),
        in_specs=[row(d), row(attn.shape[1]), row(ssm.shape[1]), row(d), row(d),
                  _resident(wa.shape), _resident(ws.shape), _resident(wo.shape),
                  _resident(g.shape), _resident(b.shape)],
        out_specs=row(d),
        out_shape=jax.ShapeDtypeStruct((n, d), F32),
        compiler_params=_cparams("parallel"),
        name="mix_out_ln",
    )(x, attn, ssm, ga, gb, wa, ws, wo, g, b)


def _ffn_kernel(x_ref, w1_ref, w3_ref, w2_ref, g_ref, b_ref, o_ref, *, alpha, f_chunk):
    x = x_ref[...]
    xb = x.astype(BF16)
    acc = jnp.zeros(x.shape, F32)
    for c in range(w1_ref.shape[1] // f_chunk):
        sl = slice(c * f_chunk, (c + 1) * f_chunk)
        h = jax.nn.silu(_dot(xb, w1_ref[:, sl])) * _dot(xb, w3_ref[:, sl])
        acc = acc + _dot(h.astype(BF16), w2_ref[sl, :])
    o_ref[...] = _layer_norm(alpha * x + acc, g_ref[...], b_ref[...])


def _ffn_chunk(d_ff):
    best = LANES
    for c in range(LANES, d_ff + 1, LANES):
        if d_ff % c == 0 and c <= 1536:
            best = c
    return best


def _ffn(x, w1, w3, w2, g, b, alpha):
    n, d = x.shape
    tm = _row_tile(n, TOKEN_TILE)
    row = pl.BlockSpec((tm, d), lambda i: (i, 0))
    return pl.pallas_call(
        functools.partial(_ffn_kernel, alpha=alpha, f_chunk=_ffn_chunk(w1.shape[1])),
        grid=(n // tm,),
        in_specs=[row, _resident(w1.shape), _resident(w3.shape), _resident(w2.shape),
                  _resident(g.shape), _resident(b.shape)],
        out_specs=row,
        out_shape=jax.ShapeDtypeStruct((n, d), F32),
        compiler_params=_cparams("parallel"),
        name="ffn_ln",
    )(x, w1, w3, w2, g, b)


def _moe_kernel(x_ref, wr_ref, br_ref, w1_ref, w3_ref, w2_ref, g_ref, b_ref, o_ref, comb_ref, acc_ref, *, alpha):
    e = pl.program_id(1)
    x = x_ref[...]
    xb = x.astype(BF16)
    n_exp = comb_ref.shape[1]
    lane = lax.broadcasted_iota(I32, comb_ref.shape, 1).astype(F32)

    @pl.when(e == 0)
    def _():
        logits = _dot(xb, wr_ref[...]) + br_ref[...]
        m1 = jnp.max(logits, axis=1, keepdims=True)
        i1 = jnp.min(jnp.where(logits == m1, lane, float(n_exp)), axis=1, keepdims=True)
        rest = jnp.where(lane == i1, -jnp.inf, logits)
        m2 = jnp.max(rest, axis=1, keepdims=True)
        i2 = jnp.min(jnp.where(rest == m2, lane, float(n_exp)), axis=1, keepdims=True)
        e2 = jnp.exp(m2 - m1)
        den = 1.0 + e2
        comb_ref[...] = jnp.where(lane == i1, 1.0 / den, 0.0) + jnp.where(lane == i2, e2 / den, 0.0)
        acc_ref[...] = jnp.zeros(acc_ref.shape, F32)

    h = jax.nn.silu(_dot(xb, w1_ref[...])) * _dot(xb, w3_ref[...])
    y = _dot(h.astype(BF16), w2_ref[...])
    ce = jnp.sum(jnp.where(lane == e.astype(F32), comb_ref[...], 0.0), axis=1, keepdims=True)
    acc_ref[...] = acc_ref[...] + ce * y

    @pl.when(e == n_exp - 1)
    def _():
        o_ref[...] = _layer_norm(alpha * x + acc_ref[...], g_ref[...], b_ref[...])


def _moe_routed_kernel(x_ref, wr_ref, br_ref, tril_ref, triu_ref, w1_ref, w3_ref, w2_ref, g_ref, b_ref, o_ref,
                       comb_ref, rankc_ref, rankr_ref, acc_ref, ytok_ref, *, alpha, rb):
    e = pl.program_id(1)
    x = x_ref[...]
    xb = x.astype(BF16)
    tm = x.shape[0]
    lane = lax.broadcasted_iota(I32, (tm, LANES), 1).astype(F32)

    @pl.when(e == 0)
    def _():
        logits = _dot(xb, wr_ref[...]) + br_ref[...]
        m1 = jnp.max(logits, axis=1, keepdims=True)
        i1 = jnp.min(jnp.where(logits == m1, lane, float(LANES)), axis=1, keepdims=True)
        rest = jnp.where(lane == i1, -jnp.inf, logits)
        m2 = jnp.max(rest, axis=1, keepdims=True)
        i2 = jnp.min(jnp.where(rest == m2, lane, float(LANES)), axis=1, keepdims=True)
        e2 = jnp.exp(m2 - m1)
        den = 1.0 + e2
        comb = jnp.where(lane == i1, 1.0 / den, 0.0) + jnp.where(lane == i2, e2 / den, 0.0)
        comb_ref[...] = comb
        sel = jnp.where(comb > 0.0, 1.0, 0.0)
        rankc_ref[...] = jnp.where(comb > 0.0, _dot(tril_ref[...], sel.astype(BF16)), -1.0)
        sel_t = jnp.transpose(sel)
        rankr_ref[...] = jnp.where(sel_t > 0.0, _dot(sel_t.astype(BF16), triu_ref[...]), -1.0)
        acc_ref[...] = jnp.zeros(acc_ref.shape, F32)

    own = lane == e.astype(F32)
    ce = jnp.sum(jnp.where(own, comb_ref[...], 0.0), axis=1, keepdims=True)
    rank_col = jnp.sum(jnp.where(own, rankc_ref[...], 0.0), axis=1, keepdims=True)
    rank_row = rankr_ref[pl.ds(e, 1), :]
    cnt = jnp.sum(jnp.where(rank_col >= 0.0, 1.0, 0.0)).astype(I32)
    ytok_ref[...] = jnp.zeros(ytok_ref.shape, F32)
    want_col = lax.broadcasted_iota(I32, (rb, 1), 0).astype(F32)
    want_row = lax.broadcasted_iota(I32, (1, rb), 1).astype(F32)

    def block(blk, carry):
        base = (blk * rb).astype(F32)
        gather = jnp.where(rank_row == want_col + base, 1.0, 0.0).astype(BF16)
        xe = _dot(gather, xb).astype(BF16)
        h = jax.nn.silu(_dot(xe, w1_ref[...])) * _dot(xe, w3_ref[...])
        ye = _dot(h.astype(BF16), w2_ref[...])
        hi = ye.astype(BF16)
        lo = (ye - hi.astype(F32)).astype(BF16)
        scatter = jnp.where(rank_col == want_row + base, 1.0, 0.0).astype(BF16)
        ytok_ref[...] = ytok_ref[...] + _dot(scatter, hi) + _dot(scatter, lo)
        return carry

    lax.fori_loop(0, (cnt + rb - 1) // rb, block, 0)
    acc_ref[...] = acc_ref[...] + ce * ytok_ref[...]

    @pl.when(e == pl.num_programs(1) - 1)
    def _():
        o_ref[...] = _layer_norm(alpha * x + acc_ref[...], g_ref[...], b_ref[...])


def _moe_routed(x, wr, br, w1, w3, w2, g, b, alpha):
    n, d = x.shape
    n_exp, _, f = w1.shape
    tm = MOE_TOKEN_TILE
    assert n % tm == 0 and n_exp <= LANES
    wr_pad = jnp.zeros((d, LANES), BF16).at[:, :n_exp].set(wr)
    br_pad = jnp.full((1, LANES), NEG_BIG, F32).at[:, :n_exp].set(br)
    before = jnp.arange(tm)[None, :] < jnp.arange(tm)[:, None]
    tril, triu = before.astype(BF16), before.T.astype(BF16)
    row = pl.BlockSpec((tm, d), lambda i, e: (i, 0), pipeline_mode=pl.Buffered(1))
    return pl.pallas_call(
        functools.partial(_moe_routed_kernel, alpha=alpha, rb=MOE_EXPERT_ROWS),
        grid=(n // tm, n_exp),
        in_specs=[row, _resident(wr_pad.shape), _resident(br_pad.shape),
                  _resident(tril.shape), _resident(triu.shape),
                  pl.BlockSpec((None, d, f), lambda i, e: (e, 0, 0)),
                  pl.BlockSpec((None, d, f), lambda i, e: (e, 0, 0)),
                  pl.BlockSpec((None, f, d), lambda i, e: (e, 0, 0)),
                  _resident(g.shape), _resident(b.shape)],
        out_specs=row,
        out_shape=jax.ShapeDtypeStruct((n, d), F32),
        scratch_shapes=[pltpu.VMEM((tm, LANES), F32), pltpu.VMEM((tm, LANES), F32),
                        pltpu.VMEM((LANES, tm), F32), pltpu.VMEM((tm, d), F32), pltpu.VMEM((tm, d), F32)],
        compiler_params=_cparams("parallel", "arbitrary"),
        name="moe_routed_ln",
    )(x, wr_pad, br_pad, tril, triu, w1, w3, w2, g, b)


def _moe(x, wr, br, w1, w3, w2, g, b, alpha):
    n, d = x.shape
    n_exp, _, f = w1.shape
    tm = _row_tile(n, 2 * TOKEN_TILE)
    row = pl.BlockSpec((tm, d), lambda i, e: (i, 0))
    return pl.pallas_call(
        functools.partial(_moe_kernel, alpha=alpha),
        grid=(n // tm, n_exp),
        in_specs=[row, _resident(wr.shape), _resident(br.shape),
                  pl.BlockSpec((None, d, f), lambda i, e: (e, 0, 0)),
                  pl.BlockSpec((None, d, f), lambda i, e: (e, 0, 0)),
                  pl.BlockSpec((None, f, d), lambda i, e: (e, 0, 0)),
                  _resident(g.shape), _resident(b.shape)],
        out_specs=row,
        out_shape=jax.ShapeDtypeStruct((n, d), F32),
        scratch_shapes=[pltpu.VMEM((tm, n_exp), F32), pltpu.VMEM((tm, d), F32)],
        compiler_params=_cparams("parallel", "arbitrary"),
        name="moe_ln",
    )(x, wr, br, w1, w3, w2, g, b)


def kernel(x_prompt, x_sample, cache_k, cache_v, cache_kidx, state_ssm_re, state_ssm_im, page_table, ln1_g, ln1_b, w_in, w_attn_proj, w_ssm_proj, w_out, ssm_a_re, ssm_a_im, ssm_log_dt, ssm_b_re, ssm_b_im, ssm_c_re, ssm_c_im, ssm_d, ssm_w_glu, ssm_b_glu, ln2_g, ln2_b, ffn_w1, ffn_w3, ffn_w2, moe_w_router, moe_b_router, moe_w1, moe_w3, moe_w2):
    batch, seq, d_model = x_prompt.shape
    dec_batch, dec_seq, _ = x_sample.shape
    assert dec_seq == 1
    depth = w_in.shape[0]
    g_ssm, n_state = ssm_a_re.shape[1:]
    ssm_width = ssm_d.shape[1]
    alpha = (2.0 * depth) ** 0.25
    in_sizes = (ATTN_WIDTH, KV_WIDTH, KV_WIDTH, N_IDX_HEADS * IDX_DIM, IDX_DIM, N_IDX_HEADS,
                ssm_width, d_model, d_model)
    assert sum(in_sizes) == w_in.shape[2]
    splits = np.cumsum(np.array(in_sizes))[:-1].tolist()

    xp = x_prompt.reshape(batch * seq, d_model).astype(F32)
    xs = x_sample.reshape(dec_batch, d_model).astype(F32)
    row2 = lambda v: v.astype(F32).reshape(1, -1)

    outs = [[] for _ in range(10)]
    for l in range(depth):
        w_pieces = [w.astype(BF16) for w in jnp.split(w_in[l], splits, axis=1)]
        wa, ws, wo = w_attn_proj[l].astype(BF16), w_ssm_proj[l].astype(BF16), w_out[l].astype(BF16)
        sp = _s5_params(ssm_a_re[l], ssm_a_im[l], ssm_log_dt[l], ssm_b_re[l], ssm_b_im[l],
                        ssm_c_re[l], ssm_c_im[l], ssm_d[l], ssm_w_glu[l], ssm_b_glu[l])
        g1, b1, g2, b2 = row2(ln1_g[l]), row2(ln1_b[l]), row2(ln2_g[l]), row2(ln2_b[l])

        q, k, v, kb, vb, qi, ki, kib, wi, u, ga, gb = _in_proj(xp, w_pieces)
        attn = _prompt_attention(q, qi, wi, kb, vb, kib, batch, seq)
        ssm, hr_p, hi_p = _s5_prompt(u, sp, batch, seq, g_ssm, n_state)
        xp = _mix(xp, attn, ssm, ga, gb, wa, ws, wo, g1, b1, alpha)
        q_s, k_s, v_s, _, _, qi_s, ki_s, _, wi_s, u_s, ga_s, gb_s = _in_proj(xs, w_pieces)
        attn_s = _sample_attention(q_s, qi_s, wi_s, k_s, v_s, ki_s, cache_k, cache_v, cache_kidx,
                                   page_table, l)
        ssm_s, hr_s, hi_s = _s5_sample(u_s, state_ssm_re[l], state_ssm_im[l], sp)
        xs = _mix(xs, attn_s, ssm_s, ga_s, gb_s, wa, ws, wo, g1, b1, alpha)
        j = l // 2
        if l % 2 == 0:
            w1, w3, w2 = ffn_w1[j].astype(BF16), ffn_w3[j].astype(BF16), ffn_w2[j].astype(BF16)
            xp = _ffn(xp, w1, w3, w2, g2, b2, alpha)
            xs = _ffn(xs, w1, w3, w2, g2, b2, alpha)
        else:
            wr, br = moe_w_router[j].astype(BF16), row2(moe_b_router[j])
            w1, w3, w2 = moe_w1[j].astype(BF16), moe_w3[j].astype(BF16), moe_w2[j].astype(BF16)
            moe_p = _moe_routed if xp.shape[0] % MOE_TOKEN_TILE == 0 else _moe
            xp = moe_p(xp, wr, br, w1, w3, w2, g2, b2, alpha)
            xs = _moe(xs, wr, br, w1, w3, w2, g2, b2, alpha)

        kv_shape = (batch, seq, N_KV_HEADS, HEAD_DIM)
        kvs_shape = (dec_batch, dec_seq, N_KV_HEADS, HEAD_DIM)
        for lst, val in zip(outs, (k.reshape(kv_shape), v.reshape(kv_shape), ki.reshape(batch, seq, IDX_DIM),
                                   hr_p, hi_p, k_s.reshape(kvs_shape), v_s.reshape(kvs_shape),
                                   ki_s.reshape(dec_batch, dec_seq, IDX_DIM), hr_s, hi_s)):
            lst.append(val)

    return (xp.reshape(batch, seq, d_model), xs.reshape(dec_batch, dec_seq, d_model),
            *[jnp.stack(o) for o in outs])
```

```python
import functools
import math

import jax
import jax.numpy as jnp
import numpy as np
from jax import lax
from jax.experimental import pallas as pl
from jax.experimental.pallas import tpu as pltpu

F32 = jnp.float32
BF16 = jnp.bfloat16
I32 = jnp.int32

N_HEADS = 8
HEAD_DIM = 128
N_KV_HEADS = 4
KV_GROUP = N_HEADS // N_KV_HEADS
ATTN_WIDTH = N_HEADS * HEAD_DIM
KV_WIDTH = N_KV_HEADS * HEAD_DIM
N_IDX_HEADS = 8
IDX_DIM = 64
TOPK_MAX = 256
SSM_GROUP = 16
SSM_STATE = 64
TOP_K_EXPERTS = 2
LN_EPS = 1e-5
NEG_BIG = -1e30

LANES = 128
SUBLANES = 8
VMEM_LIMIT_BYTES = 56 * 1024 * 1024

TOKEN_TILE = 256
Q_TILE = 512
SCORE_ROW_BLOCK = 256
ATTN_ROW_BLOCK = 128
KEY_CHUNK = 512
COUNT_ROW_BLOCK = 128
SSM_CHUNK = 16
SSM_ROW_TILE = 128
SSM_GROUPS_PER_BLOCK = LANES // SSM_GROUP
SSM_BLOCK_STATES = SSM_GROUPS_PER_BLOCK * SSM_STATE
PAGES_PER_STEP = 16
MOE_TOKEN_TILE = 512
MOE_EXPERT_ROWS = 128

INT_MIN = -(2 ** 31)
KEY_NEG_INF = int(np.int32(np.uint32(0xFF800000)) ^ np.int32(0x7FFFFFFF))


def _cparams(*sem):
    return pltpu.CompilerParams(dimension_semantics=sem, vmem_limit_bytes=VMEM_LIMIT_BYTES)


def _resident(shape):
    nd = len(shape)
    return pl.BlockSpec(tuple(shape), lambda *_: (0,) * nd, pipeline_mode=pl.Buffered(1))


def _row_tile(n, want):
    t = min(n, want)
    assert n % t == 0, (n, t)
    return t


def _layer_norm(x, g, b):
    mu = jnp.mean(x, axis=-1, keepdims=True)
    xc = x - mu
    var = jnp.mean(xc * xc, axis=-1, keepdims=True)
    return xc * lax.rsqrt(var + LN_EPS) * g + b


def _dot(a, b):
    return jnp.dot(a, b, preferred_element_type=F32)


def _dot_nt(a, b):
    return lax.dot_general(a, b, (((1,), (1,)), ((), ())), preferred_element_type=F32)


def _in_proj_kernel(x_ref, wq, wk, wv, wqi, wki, wwi, wu, wga, wgb,
                    q_o, k_o, v_o, kb_o, vb_o, qi_o, ki_o, kib_o, wi_o, u_o, ga_o, gb_o):
    xb = x_ref[...].astype(BF16)
    q = _dot(xb, wq[...]).astype(BF16)
    for h in range(N_HEADS):
        q_o[h] = q[:, h * HEAD_DIM:(h + 1) * HEAD_DIM]
    k = _dot(xb, wk[...])
    k_o[...] = k
    kb_o[...] = k.astype(BF16)
    v = _dot(xb, wv[...])
    v_o[...] = v
    vb = v.astype(BF16)
    for g in range(N_KV_HEADS):
        vb_o[g] = vb[:, g * HEAD_DIM:(g + 1) * HEAD_DIM]
    qi = _dot(xb, wqi[...]).astype(BF16)
    for h in range(N_IDX_HEADS):
        qi_o[h] = qi[:, h * IDX_DIM:(h + 1) * IDX_DIM]
    ki = _dot(xb, wki[...])
    ki_o[...] = ki
    kib_o[...] = ki.astype(BF16)
    wi_o[...] = _dot(xb, wwi[...])
    u_o[...] = _dot(xb, wu[...])
    ga_o[...] = _dot(xb, wga[...])
    gb_o[...] = _dot(xb, wgb[...])


def _in_proj(x, ws):
    n, d = x.shape
    tm = _row_tile(n, TOKEN_TILE)
    widths = [w.shape[1] for w in ws]
    aw, kvw, _, qiw, kiw, wiw, uw, gw, _ = widths
    row = lambda w: pl.BlockSpec((tm, w), lambda i: (i, 0))
    out_shape = [
        jax.ShapeDtypeStruct((N_HEADS, n, HEAD_DIM), BF16),
        jax.ShapeDtypeStruct((n, kvw), F32),
        jax.ShapeDtypeStruct((n, kvw), F32),
        jax.ShapeDtypeStruct((n, kvw), BF16),
        jax.ShapeDtypeStruct((N_KV_HEADS, n, HEAD_DIM), BF16),
        jax.ShapeDtypeStruct((N_IDX_HEADS, n, IDX_DIM), BF16),
        jax.ShapeDtypeStruct((n, kiw), F32),
        jax.ShapeDtypeStruct((n, kiw), BF16),
        jax.ShapeDtypeStruct((n, wiw), F32),
        jax.ShapeDtypeStruct((n, uw), F32),
        jax.ShapeDtypeStruct((n, gw), F32),
        jax.ShapeDtypeStruct((n, gw), F32),
    ]
    out_specs = [pl.BlockSpec((N_HEADS, tm, HEAD_DIM), lambda i: (0, i, 0)),
                 row(kvw), row(kvw), row(kvw),
                 pl.BlockSpec((N_KV_HEADS, tm, HEAD_DIM), lambda i: (0, i, 0)),
                 pl.BlockSpec((N_IDX_HEADS, tm, IDX_DIM), lambda i: (0, i, 0)),
                 row(kiw), row(kiw), row(wiw), row(uw), row(gw), row(gw)]
    return pl.pallas_call(
        _in_proj_kernel,
        grid=(n // tm,),
        in_specs=[row(d)] + [_resident(w.shape) for w in ws],
        out_specs=out_specs,
        out_shape=out_shape,
        compiler_params=_cparams("parallel"),
        name="in_proj",
    )(x, *ws)


def _score_to_key(score):
    score = jnp.where(score == 0.0, 0.0, score)
    bits = lax.bitcast_convert_type(score, I32)
    return bits ^ ((bits >> 31) & 0x7FFFFFFF)


def _select_topk(keys_ref, rows, nkc, topk, idx_bits):
    ck = KEY_CHUNK
    rb = min(rows, COUNT_ROW_BLOCK)
    assert rows % rb == 0
    row_blocks = [slice(r0, r0 + rb) for r0 in range(0, rows, rb)]
    lane = lax.broadcasted_iota(I32, (rb, ck), 1)

    def count(pred):
        parts = []
        for rs in row_blocks:
            def body(c, acc, rs=rs):
                off = pl.multiple_of(c * ck, ck)
                m = jnp.where(pred(keys_ref[rs, pl.ds(off, ck)], off, rs), 1.0, 0.0)
                part = m[:, 0:LANES]
                for t in range(1, ck // LANES):
                    part = part + m[:, t * LANES:(t + 1) * LANES]
                return acc + part
            parts.append(lax.fori_loop(0, nkc, body, jnp.zeros((rb, LANES), F32)))
        acc = parts[0] if len(parts) == 1 else jnp.concatenate(parts, axis=0)
        return jnp.sum(acc, axis=1, keepdims=True)

    kf = float(topk)
    t0 = jnp.where(count(lambda kc, off, rs: kc >= 0) >= kf, 0, INT_MIN).astype(I32)

    def bit_body(j, t):
        cand = t + jnp.left_shift(jnp.int32(1), 30 - j)
        return jnp.where(count(lambda kc, off, rs: kc >= cand[rs]) >= kf, cand, t)

    t = lax.fori_loop(0, 31, bit_body, t0)
    t = jnp.maximum(t, KEY_NEG_INF + 1)

    cnt_gt = count(lambda kc, off, rs: kc > t[rs])
    cnt_ge = count(lambda kc, off, rs: kc >= t[rs])
    need = kf - cnt_gt
    tie = (cnt_ge - cnt_gt) > need

    @pl.when(jnp.max(jnp.where(tie, 1.0, 0.0)) > 0.0)
    def _():
        def idx_body(j, m):
            cand = m + jnp.left_shift(jnp.int32(1), idx_bits - 1 - j)
            c = count(lambda kc, off, rs: jnp.where(kc == t[rs], off + lane, cand[rs]) < cand[rs])
            return jnp.where(c < need, cand, m)

        m = lax.fori_loop(0, idx_bits, idx_body, jnp.zeros((rows, 1), I32))
        last = jnp.where(tie, m, 2 ** 30)

        for rs in row_blocks:
            def fix(c, carry, rs=rs):
                off = pl.multiple_of(c * ck, ck)
                kc = keys_ref[rs, pl.ds(off, ck)]
                lose = jnp.where(kc == t[rs], off + lane, -1) > last[rs]
                keys_ref[rs, pl.ds(off, ck)] = jnp.where(lose, kc - 1, kc)
                return carry

            lax.fori_loop(0, nkc, fix, 0)

    return t


def _prompt_attn_kernel(q_ref, qi_ref, wi_ref, kt_ref, v_ref, kit_ref, o_ref,
                        keys_ref, m_ref, l_ref, acc_ref, alpha_ref, s_ref, p_ref, *, topk, idx_bits):
    tq, ck = Q_TILE, KEY_CHUNK
    i = pl.program_id(1)
    row0 = i * tq
    nkc = (row0 + tq + ck - 1) // ck
    sb = min(tq, SCORE_ROW_BLOCK)
    row = row0 + lax.broadcasted_iota(I32, (sb, ck), 0)
    lane = lax.broadcasted_iota(I32, (sb, ck), 1)

    def score_body(c, carry):
        off = pl.multiple_of(c * ck, ck)
        kic = kit_ref[:, pl.ds(off, ck)]
        for r0 in range(0, tq, sb):
            rs = slice(r0, r0 + sb)
            wi = wi_ref[rs, :]
            acc = jnp.zeros((sb, ck), F32)
            for h in range(N_IDX_HEADS):
                d = _dot(qi_ref[h, rs, :], kic)
                acc = acc + wi[:, h:h + 1] * jnp.maximum(d, 0.0)
            acc = jnp.where(off + lane <= row + r0, acc, -jnp.inf)
            keys_ref[rs, pl.ds(off, ck)] = _score_to_key(acc)
        return carry

    lax.fori_loop(0, nkc, score_body, 0)
    t = _select_topk(keys_ref, tq, nkc, topk, idx_bits)

    c2 = (HEAD_DIM ** -0.5) * math.log2(math.e)
    rb = ATTN_ROW_BLOCK

    for r0 in range(0, tq, rb):
        rs = slice(r0, r0 + rb)
        n_units = ((row0 + r0 + rb + ck - 1) // ck) * N_KV_HEADS
        t_b = t[rs]
        m_ref[...] = jnp.full(m_ref.shape, NEG_BIG, F32)
        l_ref[...] = jnp.zeros(l_ref.shape, F32)
        acc_ref[...] = jnp.zeros(acc_ref.shape, F32)
        s_ref[...] = jnp.full(s_ref.shape, NEG_BIG, F32)
        p_ref[...] = jnp.zeros(p_ref.shape, BF16)
        alpha_ref[...] = jnp.ones(alpha_ref.shape, F32)

        def attn_step(n, carry, rs=rs, n_units=n_units, t_b=t_b):
            off_c = pl.multiple_of((jnp.maximum(n - 2, 0) // N_KV_HEADS) * ck, ck)
            g_c = (n + 2) % N_KV_HEADS
            slot_c = n % 2
            vc = v_ref[g_c, pl.ds(off_c, ck), :]
            for j in range(KV_GROUP):
                h = g_c * KV_GROUP + j
                acc_ref[h] = alpha_ref[slot_c, j] * acc_ref[h] + _dot(p_ref[slot_c, j], vc)
            g_b = (n + 3) % N_KV_HEADS
            slot_b = (n + 1) % 2
            for j in range(KV_GROUP):
                h = g_b * KV_GROUP + j
                s = s_ref[slot_b, j]
                m_old = m_ref[h]
                m_new = jnp.maximum(m_old, jnp.max(s, axis=1, keepdims=True))
                p = jnp.exp2((s - m_new) * c2)
                alpha = jnp.exp2((m_old - m_new) * c2)
                l_ref[h] = alpha * l_ref[h] + jnp.sum(p, axis=1, keepdims=True)
                p_ref[slot_b, j] = p.astype(BF16)
                alpha_ref[slot_b, j] = alpha
                m_ref[h] = m_new
            unit_a = jnp.minimum(n, n_units - 1)
            off_a = pl.multiple_of((unit_a // N_KV_HEADS) * ck, ck)
            g_a = unit_a % N_KV_HEADS
            t_n = jnp.where(n < n_units, t_b, 2 ** 31 - 1)
            bias = jnp.where(keys_ref[rs, pl.ds(off_a, ck)] >= t_n, 0.0, NEG_BIG)
            kc = kt_ref[g_a, :, pl.ds(off_a, ck)]
            for j in range(KV_GROUP):
                s_ref[slot_c, j] = _dot(q_ref[g_a * KV_GROUP + j, rs, :], kc) + bias
            return carry

        lax.fori_loop(0, n_units + 2, attn_step, 0)
        for h in range(N_HEADS):
            o_ref[rs, h * HEAD_DIM:(h + 1) * HEAD_DIM] = (acc_ref[h] / l_ref[h]).astype(o_ref.dtype)


def _prompt_attention(q, qi, wi, kb, vb, kib, batch, seq):
    topk = min(TOPK_MAX, seq // 4)
    assert seq % KEY_CHUNK == 0 and seq % Q_TILE == 0
    idx_bits = max(1, (seq - 1).bit_length())
    q4 = q.reshape(N_HEADS, batch, seq, HEAD_DIM)
    qi4 = qi.reshape(N_IDX_HEADS, batch, seq, IDX_DIM)
    wi3 = wi.reshape(batch, seq, N_IDX_HEADS)
    kt4 = jnp.swapaxes(kb.reshape(batch, seq, KV_WIDTH), 1, 2).reshape(batch, N_KV_HEADS, HEAD_DIM, seq)
    v4 = vb.reshape(N_KV_HEADS, batch, seq, HEAD_DIM)
    kit3 = jnp.swapaxes(kib.reshape(batch, seq, IDX_DIM), 1, 2)
    tq, rb = Q_TILE, ATTN_ROW_BLOCK
    assert tq % rb == 0
    out = pl.pallas_call(
        functools.partial(_prompt_attn_kernel, topk=topk, idx_bits=idx_bits),
        grid=(batch, seq // tq),
        in_specs=[
            pl.BlockSpec((N_HEADS, None, tq, HEAD_DIM), lambda b, i: (0, b, i, 0)),
            pl.BlockSpec((N_IDX_HEADS, None, tq, IDX_DIM), lambda b, i: (0, b, i, 0)),
            pl.BlockSpec((None, tq, N_IDX_HEADS), lambda b, i: (b, i, 0)),
            pl.BlockSpec((None, N_KV_HEADS, HEAD_DIM, seq), lambda b, i: (b, 0, 0, 0)),
            pl.BlockSpec((N_KV_HEADS, None, seq, HEAD_DIM), lambda b, i: (0, b, 0, 0)),
            pl.BlockSpec((None, IDX_DIM, seq), lambda b, i: (b, 0, 0)),
        ],
        out_specs=pl.BlockSpec((None, tq, ATTN_WIDTH), lambda b, i: (b, i, 0)),
        out_shape=jax.ShapeDtypeStruct((batch, seq, ATTN_WIDTH), BF16),
        scratch_shapes=[pltpu.VMEM((tq, seq), I32),
                        pltpu.VMEM((N_HEADS, rb, 1), F32),
                        pltpu.VMEM((N_HEADS, rb, 1), F32),
                        pltpu.VMEM((N_HEADS, rb, HEAD_DIM), F32),
                        pltpu.VMEM((2, KV_GROUP, rb, 1), F32),
                        pltpu.VMEM((2, KV_GROUP, rb, KEY_CHUNK), F32),
                        pltpu.VMEM((2, KV_GROUP, rb, KEY_CHUNK), BF16)],
        compiler_params=_cparams("parallel", "arbitrary"),
        name="prompt_attention",
    )(q4, qi4, wi3, kt4, v4, kit3)
    return out.reshape(batch * seq, ATTN_WIDTH)


def _sample_scores_kernel(pt_ref, qi_ref, wi_ref, kin_ref, *rest, n_groups):
    page_refs, keys_ref = rest[:PAGES_PER_STEP], rest[PAGES_PER_STEP]
    j = pl.program_id(1)
    qi = qi_ref[...]
    wi = wi_ref[...]

    @pl.when(j < n_groups)
    def _():
        for p in range(PAGES_PER_STEP):
            d = _dot(qi, page_refs[p][...].astype(BF16))
            s = jnp.sum(wi * jnp.maximum(d, 0.0), axis=0, keepdims=True)
            keys_ref[p:p + 1, :] = _score_to_key(s)

    @pl.when(j == n_groups)
    def _():
        kin = kin_ref[...].astype(BF16).astype(F32)
        d = jnp.sum(qi.astype(F32) * kin, axis=1, keepdims=True)
        s = jnp.sum(wi * jnp.maximum(d, 0.0), axis=0, keepdims=True)
        first = (lax.broadcasted_iota(I32, keys_ref.shape, 0) == 0) & (
            lax.broadcasted_iota(I32, keys_ref.shape, 1) == 0)
        keys_ref[...] = jnp.where(first, _score_to_key(s), KEY_NEG_INF)


def _sample_select_kernel(keys_in_ref, tri_ref, digits_ref, pos_ref, cnt_ref, keys_ref, rank_ref,
                          *, topk, idx_bits, past):
    rows, width = keys_in_ref.shape
    ck = KEY_CHUNK
    keys_ref[...] = keys_in_ref[...]
    t = _select_topk(keys_ref, rows, width // ck, topk, idx_bits)

    rank_ref[...] = jnp.full(rank_ref.shape, -1.0, F32)
    carry = jnp.zeros((rows, 1), F32)
    for c in range(past // LANES):
        sl = slice(c * LANES, (c + 1) * LANES)
        m = jnp.where(keys_ref[:, sl] >= t, 1.0, 0.0)
        incl = _dot(m.astype(BF16), tri_ref[...])
        rank_ref[:, sl] = jnp.where(m > 0.0, carry + incl - 1.0, -1.0)
        carry = carry + incl[:, LANES - 1:LANES]
    cnt_ref[...] = carry

    want = lax.broadcasted_iota(I32, (topk, 1), 0).astype(F32)

    def row_body(b, carry_):
        acc = jnp.zeros((topk, LANES), F32)
        for c in range(past // ck):
            r = rank_ref[pl.ds(b, 1), c * ck:(c + 1) * ck]
            onehot = jnp.where(r == want, 1.0, 0.0).astype(BF16)
            acc = acc + _dot(onehot, digits_ref[c * ck:(c + 1) * ck, :])
        pos_ref[b] = acc
        return carry_

    lax.fori_loop(0, rows, row_body, 0)


def _sample_attend_kernel(pos_ref, cnt_ref, pt_ref, q_ref, kn_ref, vn_ref, ck_hbm, cv_hbm, o_ref,
                          kbuf, vbuf, sem, *, layer, n_pages, page, topk):
    b = pl.program_id(0)
    kvh = N_KV_HEADS

    def row_copies(bb, i, slot):
        pos = pos_ref[bb * topk + i]
        pg = pt_ref[bb * n_pages + pos // page]
        src = pl.ds(pl.multiple_of((pos % page) * kvh, kvh), kvh)
        dst = pl.ds(pl.multiple_of(i * kvh, kvh), kvh)
        return (pltpu.make_async_copy(ck_hbm.at[layer, pg, src, :], kbuf.at[slot, dst, :], sem.at[0, slot]),
                pltpu.make_async_copy(cv_hbm.at[layer, pg, src, :], vbuf.at[slot, dst, :], sem.at[1, slot]))

    def start_all(bb, slot):
        def body(i, carry):
            for cp in row_copies(bb, i, slot):
                cp.start()
            return carry
        lax.fori_loop(0, topk, body, 0)

    @pl.when(b == 0)
    def _():
        start_all(0, 0)

    @pl.when(b + 1 < pl.num_programs(0))
    def _():
        start_all(b + 1, (b + 1) % 2)

    slot = b % 2

    def wait_body(i, carry):
        for cp in row_copies(b, i, slot):
            cp.wait()
        return carry

    lax.fori_loop(0, topk, wait_body, 0)

    scale = HEAD_DIM ** -0.5
    q = q_ref[...]
    n_rows = topk * kvh
    col = lax.broadcasted_iota(I32, (N_HEADS, n_rows), 1)
    head_group = lax.broadcasted_iota(I32, (N_HEADS, n_rows), 0) // KV_GROUP
    cnt = cnt_ref[b]
    valid = (col % kvh == head_group) & (col // kvh < cnt)
    s = jnp.where(valid, _dot_nt(q, kbuf[slot].astype(BF16)) * scale, NEG_BIG)

    kn = kn_ref[...].astype(BF16).astype(F32)
    vn = vn_ref[...].astype(BF16).astype(F32)
    qf = q.astype(F32)
    row_group = lax.broadcasted_iota(I32, (N_HEADS, 1), 0) // KV_GROUP
    s_new = jnp.zeros((N_HEADS, 1), F32)
    v_new = jnp.zeros((N_HEADS, HEAD_DIM), F32)
    for g in range(kvh):
        sl = slice(g * HEAD_DIM, (g + 1) * HEAD_DIM)
        s_new = jnp.where(row_group == g, jnp.sum(qf * kn[:, sl], axis=1, keepdims=True), s_new)
        v_new = jnp.where(row_group == g, vn[:, sl], v_new)
    new_sel = cnt < topk
    s_new = jnp.where(new_sel, s_new * scale, NEG_BIG)

    m = jnp.maximum(jnp.max(s, axis=1, keepdims=True), s_new)
    p = jnp.where(valid, jnp.exp(s - m), 0.0)
    p_new = jnp.where(new_sel, jnp.exp(s_new - m), 0.0)
    l = jnp.sum(p, axis=1, keepdims=True) + p_new
    acc = _dot(p.astype(BF16), vbuf[slot].astype(BF16)) + p_new.astype(BF16).astype(F32) * v_new
    o_ref[...] = (acc / l).astype(o_ref.dtype)


def _sample_attention(q, qi, wi, k_new, v_new, ki_new, cache_k, cache_v, cache_kidx, page_table, layer):
    db, n_pages = page_table.shape
    page = cache_k.shape[2]
    npg = PAGES_PER_STEP
    assert n_pages % npg == 0 and page == LANES
    n_groups = n_pages // npg
    past = n_pages * page
    topk = min(TOPK_MAX, (past + 1) // 4)
    n_rows = n_pages + npg
    width = n_rows * page
    assert width % KEY_CHUNK == 0
    idx_bits = (width - 1).bit_length()
    pt = page_table.reshape(-1).astype(I32)

    page_rows = page * N_KV_HEADS
    ck4 = cache_k.reshape(cache_k.shape[0], cache_k.shape[1], page_rows, HEAD_DIM)
    cv4 = cache_v.reshape(cache_v.shape[0], cache_v.shape[1], page_rows, HEAD_DIM)
    kidx_t = jnp.swapaxes(cache_kidx, 2, 3)
    qi3 = jnp.transpose(qi, (1, 0, 2))
    wi3 = wi.reshape(db, N_IDX_HEADS, 1)
    q3 = jnp.transpose(q, (1, 0, 2))

    def page_spec(rows_, width_, p):
        def imap(b, j, pt_ref):
            pg = jnp.minimum(j * npg + p, n_pages - 1)
            return (layer, pt_ref[b * n_pages + pg], 0, 0)
        return pl.BlockSpec((None, None, rows_, width_), imap)

    keys = pl.pallas_call(
        functools.partial(_sample_scores_kernel, n_groups=n_groups),
        grid_spec=pltpu.PrefetchScalarGridSpec(
            num_scalar_prefetch=1,
            grid=(db, n_groups + 1),
            in_specs=[
                pl.BlockSpec((None, N_IDX_HEADS, IDX_DIM), lambda b, j, pt_ref: (b, 0, 0)),
                pl.BlockSpec((None, N_IDX_HEADS, 1), lambda b, j, pt_ref: (b, 0, 0)),
                pl.BlockSpec((None, 1, IDX_DIM), lambda b, j, pt_ref: (b, 0, 0)),
            ] + [page_spec(IDX_DIM, page, p) for p in range(npg)],
            out_specs=pl.BlockSpec((None, npg, page), lambda b, j, pt_ref: (b, j, 0)),
        ),
        out_shape=jax.ShapeDtypeStruct((db, n_rows, page), I32),
        compiler_params=_cparams("parallel", "arbitrary"),
        name="sample_scores",
    )(pt, qi3, wi3, ki_new.reshape(db, 1, IDX_DIM), *([kidx_t] * npg))

    tri = (jnp.arange(LANES)[:, None] <= jnp.arange(LANES)[None, :]).astype(BF16)
    positions = jnp.arange(past)
    digits = jnp.zeros((past, LANES), BF16).at[:, 0].set((positions // LANES).astype(BF16))
    digits = digits.at[:, 1].set((positions % LANES).astype(BF16))
    pos_digits, cnt = pl.pallas_call(
        functools.partial(_sample_select_kernel, topk=topk, idx_bits=idx_bits, past=past),
        out_shape=[jax.ShapeDtypeStruct((db, topk, LANES), F32), jax.ShapeDtypeStruct((db, 1), F32)],
        scratch_shapes=[pltpu.VMEM((db, width), I32), pltpu.VMEM((db, width), F32)],
        compiler_params=pltpu.CompilerParams(vmem_limit_bytes=VMEM_LIMIT_BYTES),
        name="sample_select",
    )(keys.reshape(db, width), tri, digits)
    pos = (pos_digits[:, :, 0] * LANES + pos_digits[:, :, 1]).astype(I32).reshape(-1)
    cnt = cnt.astype(I32).reshape(-1)

    out = pl.pallas_call(
        functools.partial(_sample_attend_kernel, layer=layer, n_pages=n_pages, page=page, topk=topk),
        grid_spec=pltpu.PrefetchScalarGridSpec(
            num_scalar_prefetch=3,
            grid=(db,),
            in_specs=[
                pl.BlockSpec((None, N_HEADS, HEAD_DIM), lambda b, *_: (b, 0, 0)),
                pl.BlockSpec((None, 1, KV_WIDTH), lambda b, *_: (b, 0, 0)),
                pl.BlockSpec((None, 1, KV_WIDTH), lambda b, *_: (b, 0, 0)),
                pl.BlockSpec(memory_space=pl.ANY),
                pl.BlockSpec(memory_space=pl.ANY),
            ],
            out_specs=pl.BlockSpec((None, N_HEADS, HEAD_DIM), lambda b, *_: (b, 0, 0)),
            scratch_shapes=[pltpu.VMEM((2, topk * N_KV_HEADS, HEAD_DIM), F32),
                            pltpu.VMEM((2, topk * N_KV_HEADS, HEAD_DIM), F32),
                            pltpu.SemaphoreType.DMA((2, 2))],
        ),
        out_shape=jax.ShapeDtypeStruct((db, N_HEADS, HEAD_DIM), BF16),
        compiler_params=_cparams("arbitrary"),
        name="sample_attend",
    )(pos, cnt, pt, q3, k_new.reshape(db, 1, KV_WIDTH), v_new.reshape(db, 1, KV_WIDTH), ck4, cv4)
    return out.reshape(db, ATTN_WIDTH)


def _s5_scan_kernel(u_ref, h0_ref, a_ref, bbd_ref, cbd_ref, d_ref, wglu_ref, bglu_ref,
                    *out_refs, n_steps, emit_y):
    if emit_y:
        y_ref, hend_ref = out_refs
    else:
        (hend_ref,) = out_refs
    width = SSM_GROUPS_PER_BLOCK * SSM_GROUP * (bbd_ref.shape[0])
    nb = bbd_ref.shape[0]
    bs = SSM_BLOCK_STATES
    h = [None] * nb
    for blk in range(nb):
        h[blk] = (h0_ref[:, blk * 2 * bs:blk * 2 * bs + bs], h0_ref[:, blk * 2 * bs + bs:(blk + 1) * 2 * bs])
    for tau in range(n_steps):
        u = u_ref[:, tau * width:(tau + 1) * width]
        ub = u.astype(BF16)
        ys = []
        for blk in range(nb):
            bu = _dot(ub[:, blk * LANES:(blk + 1) * LANES], bbd_ref[blk])
            ar = a_ref[0:1, blk * 2 * bs:blk * 2 * bs + bs]
            ai = a_ref[0:1, blk * 2 * bs + bs:(blk + 1) * 2 * bs]
            hr, hi = h[blk]
            nr = ar * hr - ai * hi + bu[:, :bs]
            ni = ar * hi + ai * hr + bu[:, bs:]
            h[blk] = (nr, ni)
            if emit_y:
                ys.append(_dot(nr.astype(BF16), cbd_ref[blk, :bs, :])
                          - _dot(ni.astype(BF16), cbd_ref[blk, bs:, :]))
        if emit_y:
            y = jnp.concatenate(ys, axis=1) + d_ref[...] * u
            y = jax.nn.gelu(y)
            z = _dot(y.astype(BF16), wglu_ref[...]) + bglu_ref[...]
            y_ref[:, tau * width:(tau + 1) * width] = (y * jax.nn.sigmoid(z)).astype(y_ref.dtype)
    for blk in range(nb):
        hend_ref[:, blk * 2 * bs:blk * 2 * bs + bs] = h[blk][0]
        hend_ref[:, blk * 2 * bs + bs:(blk + 1) * 2 * bs] = h[blk][1]


def _s5_scan(u_rows, h0, p, n_steps, emit_y):
    r = u_rows.shape[0]
    tr = _row_tile(r, SSM_ROW_TILE)
    hw = h0.shape[1]
    row = lambda w: pl.BlockSpec((tr, w), lambda i: (i, 0))
    out_shape = [jax.ShapeDtypeStruct((r, hw), F32)]
    out_specs = [row(hw)]
    if emit_y:
        out_shape = [jax.ShapeDtypeStruct(u_rows.shape, BF16)] + out_shape
        out_specs = [row(u_rows.shape[1])] + out_specs
    consts = [p["a"], p["bbd"], p["cbd"], p["d"], p["wglu"], p["bglu"]]
    return pl.pallas_call(
        functools.partial(_s5_scan_kernel, n_steps=n_steps, emit_y=emit_y),
        grid=(r // tr,),
        in_specs=[row(u_rows.shape[1]), row(hw)] + [_resident(c.shape) for c in consts],
        out_specs=out_specs,
        out_shape=out_shape,
        compiler_params=_cparams("parallel"),
        name="s5_scan_y" if emit_y else "s5_scan_state",
    )(u_rows, h0, *consts)


def _s5_carry_kernel(s_ref, a_ref, hprev_ref, hend_ref, *, n_chunks, chunk_len):
    hw = s_ref.shape[1]
    piece = SSM_BLOCK_STATES
    for blk in range(hw // (2 * piece)):
        re = pl.ds(blk * 2 * piece, piece)
        im = pl.ds(blk * 2 * piece + piece, piece)
        ar, ai = a_ref[0:1, re], a_ref[0:1, im]
        pr, pi = ar, ai
        for _ in range(chunk_len - 1):
            pr, pi = pr * ar - pi * ai, pr * ai + pi * ar

        def body(c, carry, re=re, im=im, pr=pr, pi=pi):
            hr, hi = carry
            hprev_ref[pl.ds(c, 1), re] = hr
            hprev_ref[pl.ds(c, 1), im] = hi
            sr = s_ref[pl.ds(c, 1), re]
            si = s_ref[pl.ds(c, 1), im]
            return pr * hr - pi * hi + sr, pr * hi + pi * hr + si

        zero = jnp.zeros((1, piece), F32)
        hr, hi = lax.fori_loop(0, n_chunks, body, (zero, zero))
        hend_ref[0:1, re] = hr
        hend_ref[0:1, im] = hi


def _s5_carry(s, a, batch, n_chunks, chunk_len):
    hw = s.shape[1]
    s3 = s.reshape(batch, n_chunks, hw)
    hprev, hend = pl.pallas_call(
        functools.partial(_s5_carry_kernel, n_chunks=n_chunks, chunk_len=chunk_len),
        grid=(batch,),
        in_specs=[pl.BlockSpec((None, n_chunks, hw), lambda b: (b, 0, 0)), _resident(a.shape)],
        out_specs=[pl.BlockSpec((None, n_chunks, hw), lambda b: (b, 0, 0)),
                   pl.BlockSpec((None, 1, hw), lambda b: (b, 0, 0))],
        out_shape=[jax.ShapeDtypeStruct((batch, n_chunks, hw), F32),
                   jax.ShapeDtypeStruct((batch, 1, hw), F32)],
        compiler_params=_cparams("parallel"),
        name="s5_carry",
    )(s3, a)
    return hprev.reshape(batch * n_chunks, hw), hend.reshape(batch, hw)


def _s5_params(a_re, a_im, log_dt, b_re, b_im, c_re, c_im, d_skip, w_glu, b_glu):
    g, p = a_re.shape
    c = b_re.shape[2]
    gpb = SSM_GROUPS_PER_BLOCK
    nb = g // gpb
    lr, li = a_re.astype(F32), a_im.astype(F32)
    dt = jnp.exp(log_dt.astype(F32))[:, None]
    mag = jnp.exp(lr * dt)
    ab_re, ab_im = mag * jnp.cos(li * dt), mag * jnp.sin(li * dt)
    den = lr * lr + li * li
    nr = ab_re - 1.0
    f_re = (nr * lr + ab_im * li) / den
    f_im = (ab_im * lr - nr * li) / den
    br, bi = b_re.astype(F32), b_im.astype(F32)
    bb_re = f_re[..., None] * br - f_im[..., None] * bi
    bb_im = f_re[..., None] * bi + f_im[..., None] * br
    eye = jnp.eye(gpb, dtype=F32)

    def state_layout(re, im):
        return jnp.concatenate([re.reshape(nb, gpb * p), im.reshape(nb, gpb * p)], axis=1).reshape(1, -1)

    def in_block(x):
        x = jnp.transpose(x.reshape(nb, gpb, p, c), (0, 1, 3, 2))
        return (x[:, :, :, None, :] * eye[None, :, None, :, None]).reshape(nb, gpb * c, gpb * p)

    def out_block(x):
        x = jnp.transpose(x.reshape(nb, gpb, c, p), (0, 1, 3, 2))
        return (x[:, :, :, None, :] * eye[None, :, None, :, None]).reshape(nb, gpb * p, gpb * c)

    bbd = jnp.concatenate([in_block(bb_re), in_block(bb_im)], axis=2).astype(BF16)
    cbd = jnp.concatenate([out_block(c_re.astype(F32)), out_block(c_im.astype(F32))], axis=1).astype(BF16)
    return {"a": state_layout(ab_re, ab_im), "bbd": bbd, "cbd": cbd,
            "d": d_skip.astype(F32).reshape(1, -1), "wglu": w_glu.astype(BF16),
            "bglu": b_glu.astype(F32).reshape(1, -1)}


def _to_state_layout(re, im):
    b, g, p = re.shape
    nb = g // SSM_GROUPS_PER_BLOCK
    return jnp.concatenate([re.reshape(b, nb, -1), im.reshape(b, nb, -1)], axis=2).reshape(b, -1)


def _from_state_layout(h, g, p):
    b = h.shape[0]
    nb = g // SSM_GROUPS_PER_BLOCK
    h4 = h.reshape(b, nb, 2, SSM_GROUPS_PER_BLOCK * p)
    return h4[:, :, 0].reshape(b, g, p), h4[:, :, 1].reshape(b, g, p)


def _s5_prompt(u, p, batch, seq, g, n_state):
    w = u.shape[1]
    lc = SSM_CHUNK
    assert seq % lc == 0
    n_chunks = seq // lc
    rows = batch * n_chunks
    u_rows = u.reshape(rows, lc * w)
    zeros = jnp.zeros((rows, p["a"].shape[1]), F32)
    (s_loc,) = _s5_scan(u_rows, zeros, p, lc, emit_y=False)
    hprev, hend = _s5_carry(s_loc, p["a"], batch, n_chunks, lc)
    y, _ = _s5_scan(u_rows, hprev, p, lc, emit_y=True)
    hr, hi = _from_state_layout(hend, g, n_state)
    return y.reshape(batch * seq, w), hr, hi


def _s5_sample(u, h0_re, h0_im, p):
    g, n_state = h0_re.shape[1:]
    y, hend = _s5_scan(u, _to_state_layout(h0_re.astype(F32), h0_im.astype(F32)), p, 1, emit_y=True)
    hr, hi = _from_state_layout(hend, g, n_state)
    return y, hr, hi


def _mix_kernel(x_ref, attn_ref, ssm_ref, ga_ref, gb_ref, wa_ref, ws_ref, wo_ref, g_ref, b_ref, o_ref, *, alpha):
    a = _dot(attn_ref[...], wa_ref[...])
    s = _dot(ssm_ref[...], ws_ref[...])
    merged = jax.nn.sigmoid(ga_ref[...]) * a + jax.nn.sigmoid(gb_ref[...]) * s
    mix = _dot(merged.astype(BF16), wo_ref[...])
    o_ref[...] = _layer_norm(alpha * x_ref[...] + mix, g_ref[...], b_ref[...])


def _mix(x, attn, ssm, ga, gb, wa, ws, wo, g, b, alpha):
    n, d = x.shape
    tm = _row_tile(n, TOKEN_TILE)
    row = lambda w: pl.BlockSpec((tm, w), lambda i: (i, 0))
    return pl.pallas_call(
        functools.partial(_mix_kernel, alpha=alpha),
        grid=(n // tm,),
        in_specs=[row(d), row(attn.shape[1]), row(ssm.shape[1]), row(d), row(d),
                  _resident(wa.shape), _resident(ws.shape), _resident(wo.shape),
                  _resident(g.shape), _resident(b.shape)],
        out_specs=row(d),
        out_shape=jax.ShapeDtypeStruct((n, d), F32),
        compiler_params=_cparams("parallel"),
        name="mix_out_ln",
    )(x, attn, ssm, ga, gb, wa, ws, wo, g, b)


def _ffn_kernel(x_ref, w1_ref, w3_ref, w2_ref, g_ref, b_ref, o_ref, *, alpha, f_chunk):
    x = x_ref[...]
    xb = x.astype(BF16)
    acc = jnp.zeros(x.shape, F32)
    for c in range(w1_ref.shape[1] // f_chunk):
        sl = slice(c * f_chunk, (c + 1) * f_chunk)
        h = jax.nn.silu(_dot(xb, w1_ref[:, sl])) * _dot(xb, w3_ref[:, sl])
        acc = acc + _dot(h.astype(BF16), w2_ref[sl, :])
    o_ref[...] = _layer_norm(alpha * x + acc, g_ref[...], b_ref[...])


def _ffn_chunk(d_ff):
    best = LANES
    for c in range(LANES, d_ff + 1, LANES):
        if d_ff % c == 0 and c <= 1536:
            best = c
    return best


def _ffn(x, w1, w3, w2, g, b, alpha):
    n, d = x.shape
    tm = _row_tile(n, TOKEN_TILE)
    row = pl.BlockSpec((tm, d), lambda i: (i, 0))
    return pl.pallas_call(
        functools.partial(_ffn_kernel, alpha=alpha, f_chunk=_ffn_chunk(w1.shape[1])),
        grid=(n // tm,),
        in_specs=[row, _resident(w1.shape), _resident(w3.shape), _resident(w2.shape),
                  _resident(g.shape), _resident(b.shape)],
        out_specs=row,
        out_shape=jax.ShapeDtypeStruct((n, d), F32),
        compiler_params=_cparams("parallel"),
        name="ffn_ln",
    )(x, w1, w3, w2, g, b)


def _moe_kernel(x_ref, wr_ref, br_ref, w1_ref, w3_ref, w2_ref, g_ref, b_ref, o_ref, comb_ref, acc_ref, *, alpha):
    e = pl.program_id(1)
    x = x_ref[...]
    xb = x.astype(BF16)
    n_exp = comb_ref.shape[1]
    lane = lax.broadcasted_iota(I32, comb_ref.shape, 1).astype(F32)

    @pl.when(e == 0)
    def _():
        logits = _dot(xb, wr_ref[...]) + br_ref[...]
        m1 = jnp.max(logits, axis=1, keepdims=True)
        i1 = jnp.min(jnp.where(logits == m1, lane, float(n_exp)), axis=1, keepdims=True)
        rest = jnp.where(lane == i1, -jnp.inf, logits)
        m2 = jnp.max(rest, axis=1, keepdims=True)
        i2 = jnp.min(jnp.where(rest == m2, lane, float(n_exp)), axis=1, keepdims=True)
        e2 = jnp.exp(m2 - m1)
        den = 1.0 + e2
        comb_ref[...] = jnp.where(lane == i1, 1.0 / den, 0.0) + jnp.where(lane == i2, e2 / den, 0.0)
        acc_ref[...] = jnp.zeros(acc_ref.shape, F32)

    h = jax.nn.silu(_dot(xb, w1_ref[...])) * _dot(xb, w3_ref[...])
    y = _dot(h.astype(BF16), w2_ref[...])
    ce = jnp.sum(jnp.where(lane == e.astype(F32), comb_ref[...], 0.0), axis=1, keepdims=True)
    acc_ref[...] = acc_ref[...] + ce * y

    @pl.when(e == n_exp - 1)
    def _():
        o_ref[...] = _layer_norm(alpha * x + acc_ref[...], g_ref[...], b_ref[...])


def _moe_routed_kernel(x_ref, wr_ref, br_ref, tril_ref, triu_ref, w1_ref, w3_ref, w2_ref, g_ref, b_ref, o_ref,
                       comb_ref, rankc_ref, rankr_ref, acc_ref, xb_ref, *, alpha, rb):
    e = pl.program_id(1)
    tm = x_ref.shape[0]
    lane = lax.broadcasted_iota(I32, (tm, LANES), 1).astype(F32)

    @pl.when(e == 0)
    def _():
        xb_ref[...] = x_ref[...].astype(BF16)
        logits = _dot(xb_ref[...], wr_ref[...]) + br_ref[...]
        m1 = jnp.max(logits, axis=1, keepdims=True)
        i1 = jnp.min(jnp.where(logits == m1, lane, float(LANES)), axis=1, keepdims=True)
        rest = jnp.where(lane == i1, -jnp.inf, logits)
        m2 = jnp.max(rest, axis=1, keepdims=True)
        i2 = jnp.min(jnp.where(rest == m2, lane, float(LANES)), axis=1, keepdims=True)
        e2 = jnp.exp(m2 - m1)
        den = 1.0 + e2
        comb = jnp.where(lane == i1, 1.0 / den, 0.0) + jnp.where(lane == i2, e2 / den, 0.0)
        comb_ref[...] = comb
        sel = jnp.where(comb > 0.0, 1.0, 0.0)
        rankc_ref[...] = jnp.where(comb > 0.0, _dot(tril_ref[...], sel.astype(BF16)), -1.0)
        sel_t = jnp.transpose(sel)
        rankr_ref[...] = jnp.where(sel_t > 0.0, _dot(sel_t.astype(BF16), triu_ref[...]), -1.0)
        acc_ref[...] = jnp.zeros(acc_ref.shape, F32)

    own = lane == e.astype(F32)
    ce = jnp.sum(jnp.where(own, comb_ref[...], 0.0), axis=1, keepdims=True)
    rank_col = jnp.sum(jnp.where(own, rankc_ref[...], 0.0), axis=1, keepdims=True)
    rank_row = rankr_ref[pl.ds(e, 1), :]
    cnt = jnp.sum(jnp.where(rank_col >= 0.0, 1.0, 0.0)).astype(I32)
    want_col = lax.broadcasted_iota(I32, (rb, 1), 0).astype(F32)
    want_row = lax.broadcasted_iota(I32, (1, rb), 1).astype(F32)

    def block(blk, carry):
        base = (blk * rb).astype(F32)
        gather = jnp.where(rank_row == want_col + base, 1.0, 0.0).astype(BF16)
        xe = _dot(gather, xb_ref[...]).astype(BF16)
        h = jax.nn.silu(_dot(xe, w1_ref[...])) * _dot(xe, w3_ref[...])
        ye = _dot(h.astype(BF16), w2_ref[...])
        hi = ye.astype(BF16)
        lo = (ye - hi.astype(F32)).astype(BF16)
        scatter = jnp.where(rank_col == want_row + base, 1.0, 0.0).astype(BF16)
        acc_ref[...] = acc_ref[...] + ce * (_dot(scatter, hi) + _dot(scatter, lo))
        return carry

    lax.fori_loop(0, (cnt + rb - 1) // rb, block, 0)

    @pl.when(e == pl.num_programs(1) - 1)
    def _():
        o_ref[...] = _layer_norm(alpha * x_ref[...] + acc_ref[...], g_ref[...], b_ref[...])


def _moe_routed(x, wr, br, w1, w3, w2, g, b, alpha):
    n, d = x.shape
    n_exp, _, f = w1.shape
    tm = MOE_TOKEN_TILE
    assert n % tm == 0 and n_exp <= LANES
    wr_pad = jnp.zeros((d, LANES), BF16).at[:, :n_exp].set(wr)
    br_pad = jnp.full((1, LANES), NEG_BIG, F32).at[:, :n_exp].set(br)
    before = jnp.arange(tm)[None, :] < jnp.arange(tm)[:, None]
    tril, triu = before.astype(BF16), before.T.astype(BF16)
    row = pl.BlockSpec((tm, d), lambda i, e: (i, 0), pipeline_mode=pl.Buffered(1))
    return pl.pallas_call(
        functools.partial(_moe_routed_kernel, alpha=alpha, rb=MOE_EXPERT_ROWS),
        grid=(n // tm, n_exp),
        in_specs=[row, _resident(wr_pad.shape), _resident(br_pad.shape),
                  _resident(tril.shape), _resident(triu.shape),
                  pl.BlockSpec((None, d, f), lambda i, e: (e, 0, 0)),
                  pl.BlockSpec((None, d, f), lambda i, e: (e, 0, 0)),
                  pl.BlockSpec((None, f, d), lambda i, e: (e, 0, 0)),
                  _resident(g.shape), _resident(b.shape)],
        out_specs=row,
        out_shape=jax.ShapeDtypeStruct((n, d), F32),
        scratch_shapes=[pltpu.VMEM((tm, LANES), F32), pltpu.VMEM((tm, LANES), F32),
                        pltpu.VMEM((LANES, tm), F32), pltpu.VMEM((tm, d), F32), pltpu.VMEM((tm, d), BF16)],
        compiler_params=_cparams("parallel", "arbitrary"),
        name="moe_routed_ln",
    )(x, wr_pad, br_pad, tril, triu, w1, w3, w2, g, b)


def _moe(x, wr, br, w1, w3, w2, g, b, alpha):
    n, d = x.shape
    n_exp, _, f = w1.shape
    tm = _row_tile(n, 2 * TOKEN_TILE)
    row = pl.BlockSpec((tm, d), lambda i, e: (i, 0))
    return pl.pallas_call(
        functools.partial(_moe_kernel, alpha=alpha),
        grid=(n // tm, n_exp),
        in_specs=[row, _resident(wr.shape), _resident(br.shape),
                  pl.BlockSpec((None, d, f), lambda i, e: (e, 0, 0)),
                  pl.BlockSpec((None, d, f), lambda i, e: (e, 0, 0)),
                  pl.BlockSpec((None, f, d), lambda i, e: (e, 0, 0)),
                  _resident(g.shape), _resident(b.shape)],
        out_specs=row,
        out_shape=jax.ShapeDtypeStruct((n, d), F32),
        scratch_shapes=[pltpu.VMEM((tm, n_exp), F32), pltpu.VMEM((tm, d), F32)],
        compiler_params=_cparams("parallel", "arbitrary"),
        name="moe_ln",
    )(x, wr, br, w1, w3, w2, g, b)


def kernel(x_prompt, x_sample, cache_k, cache_v, cache_kidx, state_ssm_re, state_ssm_im, page_table, ln1_g, ln1_b, w_in, w_attn_proj, w_ssm_proj, w_out, ssm_a_re, ssm_a_im, ssm_log_dt, ssm_b_re, ssm_b_im, ssm_c_re, ssm_c_im, ssm_d, ssm_w_glu, ssm_b_glu, ln2_g, ln2_b, ffn_w1, ffn_w3, ffn_w2, moe_w_router, moe_b_router, moe_w1, moe_w3, moe_w2):
    batch, seq, d_model = x_prompt.shape
    dec_batch, dec_seq, _ = x_sample.shape
    assert dec_seq == 1
    depth = w_in.shape[0]
    g_ssm, n_state = ssm_a_re.shape[1:]
    ssm_width = ssm_d.shape[1]
    alpha = (2.0 * depth) ** 0.25
    in_sizes = (ATTN_WIDTH, KV_WIDTH, KV_WIDTH, N_IDX_HEADS * IDX_DIM, IDX_DIM, N_IDX_HEADS,
                ssm_width, d_model, d_model)
    assert sum(in_sizes) == w_in.shape[2]
    splits = np.cumsum(np.array(in_sizes))[:-1].tolist()

    xp = x_prompt.reshape(batch * seq, d_model).astype(F32)
    xs = x_sample.reshape(dec_batch, d_model).astype(F32)
    row2 = lambda v: v.astype(F32).reshape(1, -1)

    outs = [[] for _ in range(10)]
    for l in range(depth):
        w_pieces = [w.astype(BF16) for w in jnp.split(w_in[l], splits, axis=1)]
        wa, ws, wo = w_attn_proj[l].astype(BF16), w_ssm_proj[l].astype(BF16), w_out[l].astype(BF16)
        sp = _s5_params(ssm_a_re[l], ssm_a_im[l], ssm_log_dt[l], ssm_b_re[l], ssm_b_im[l],
                        ssm_c_re[l], ssm_c_im[l], ssm_d[l], ssm_w_glu[l], ssm_b_glu[l])
        g1, b1, g2, b2 = row2(ln1_g[l]), row2(ln1_b[l]), row2(ln2_g[l]), row2(ln2_b[l])

        q, k, v, kb, vb, qi, ki, kib, wi, u, ga, gb = _in_proj(xp, w_pieces)
        attn = _prompt_attention(q, qi, wi, kb, vb, kib, batch, seq)
        ssm, hr_p, hi_p = _s5_prompt(u, sp, batch, seq, g_ssm, n_state)
        xp = _mix(xp, attn, ssm, ga, gb, wa, ws, wo, g1, b1, alpha)
        q_s, k_s, v_s, _, _, qi_s, ki_s, _, wi_s, u_s, ga_s, gb_s = _in_proj(xs, w_pieces)
        attn_s = _sample_attention(q_s, qi_s, wi_s, k_s, v_s, ki_s, cache_k, cache_v, cache_kidx,
                                   page_table, l)
        ssm_s, hr_s, hi_s = _s5_sample(u_s, state_ssm_re[l], state_ssm_im[l], sp)
        xs = _mix(xs, attn_s, ssm_s, ga_s, gb_s, wa, ws, wo, g1, b1, alpha)
        j = l // 2
        if l % 2 == 0:
            w1, w3, w2 = ffn_w1[j].astype(BF16), ffn_w3[j].astype(BF16), ffn_w2[j].astype(BF16)
            xp = _ffn(xp, w1, w3, w2, g2, b2, alpha)
            xs = _ffn(xs, w1, w3, w2, g2, b2, alpha)
        else:
            wr, br = moe_w_router[j].astype(BF16), row2(moe_b_router[j])
            w1, w3, w2 = moe_w1[j].astype(BF16), moe_w3[j].astype(BF16), moe_w2[j].astype(BF16)
            moe_p = _moe_routed if xp.shape[0] % MOE_TOKEN_TILE == 0 else _moe
            xp = moe_p(xp, wr, br, w1, w3, w2, g2, b2, alpha)
            xs = _moe(xs, wr, br, w1, w3, w2, g2, b2, alpha)

        kv_shape = (batch, seq, N_KV_HEADS, HEAD_DIM)
        kvs_shape = (dec_batch, dec_seq, N_KV_HEADS, HEAD_DIM)
        for lst, val in zip(outs, (k.reshape(kv_shape), v.reshape(kv_shape), ki.reshape(batch, seq, IDX_DIM),
                                   hr_p, hi_p, k_s.reshape(kvs_shape), v_s.reshape(kvs_shape),
                                   ki_s.reshape(dec_batch, dec_seq, IDX_DIM), hr_s, hi_s)):
            lst.append(val)

    return (xp.reshape(batch, seq, d_model), xs.reshape(dec_batch, dec_seq, d_model),
            *[jnp.stack(o) for o in outs])
```

```python
import functools
import math

import jax
import jax.numpy as jnp
import numpy as np
from jax import lax
from jax.experimental import pallas as pl
from jax.experimental.pallas import tpu as pltpu

F32 = jnp.float32
BF16 = jnp.bfloat16
I32 = jnp.int32

N_HEADS = 8
HEAD_DIM = 128
N_KV_HEADS = 4
KV_GROUP = N_HEADS // N_KV_HEADS
ATTN_WIDTH = N_HEADS * HEAD_DIM
KV_WIDTH = N_KV_HEADS * HEAD_DIM
N_IDX_HEADS = 8
IDX_DIM = 64
TOPK_MAX = 256
SSM_GROUP = 16
SSM_STATE = 64
TOP_K_EXPERTS = 2
LN_EPS = 1e-5
NEG_BIG = -1e30

LANES = 128
SUBLANES = 8
VMEM_LIMIT_BYTES = 56 * 1024 * 1024

TOKEN_TILE = 512
Q_TILE = 256
SCORE_ROW_BLOCK = 256
ATTN_ROW_BLOCK = 128
KEY_CHUNK = 512
COUNT_ROW_BLOCK = 128
SSM_CHUNK = 16
SSM_ROW_TILE = 128
SSM_GROUPS_PER_BLOCK = LANES // SSM_GROUP
SSM_BLOCK_STATES = SSM_GROUPS_PER_BLOCK * SSM_STATE
PAGES_PER_STEP = 16
MOE_TOKEN_TILE = 512
MOE_EXPERT_ROWS = 128

INT_MIN = -(2 ** 31)
KEY_NEG_INF = int(np.int32(np.uint32(0xFF800000)) ^ np.int32(0x7FFFFFFF))


def _cparams(*sem):
    return pltpu.CompilerParams(dimension_semantics=sem, vmem_limit_bytes=VMEM_LIMIT_BYTES)


def _resident(shape):
    nd = len(shape)
    return pl.BlockSpec(tuple(shape), lambda *_: (0,) * nd, pipeline_mode=pl.Buffered(1))


def _row_tile(n, want):
    t = min(n, want)
    assert n % t == 0, (n, t)
    return t


def _layer_norm(x, g, b):
    mu = jnp.mean(x, axis=-1, keepdims=True)
    xc = x - mu
    var = jnp.mean(xc * xc, axis=-1, keepdims=True)
    return xc * lax.rsqrt(var + LN_EPS) * g + b


def _dot(a, b):
    return jnp.dot(a, b, preferred_element_type=F32)


def _dot_nt(a, b):
    return lax.dot_general(a, b, (((1,), (1,)), ((), ())), preferred_element_type=F32)


def _in_proj_kernel(x_ref, wq, wk, wv, wqi, wki, wwi, wu, wga, wgb,
                    q_o, k_o, v_o, kb_o, vb_o, qi_o, ki_o, kib_o, wi_o, u_o, ga_o, gb_o):
    xb = x_ref[...].astype(BF16)
    q = _dot(xb, wq[...]).astype(BF16)
    for h in range(N_HEADS):
        q_o[h] = q[:, h * HEAD_DIM:(h + 1) * HEAD_DIM]
    k = _dot(xb, wk[...])
    k_o[...] = k
    kb_o[...] = k.astype(BF16)
    v = _dot(xb, wv[...])
    v_o[...] = v
    vb = v.astype(BF16)
    for g in range(N_KV_HEADS):
        vb_o[g] = vb[:, g * HEAD_DIM:(g + 1) * HEAD_DIM]
    qi = _dot(xb, wqi[...]).astype(BF16)
    for h in range(N_IDX_HEADS):
        qi_o[h] = qi[:, h * IDX_DIM:(h + 1) * IDX_DIM]
    ki = _dot(xb, wki[...])
    ki_o[...] = ki
    kib_o[...] = ki.astype(BF16)
    wi_o[...] = _dot(xb, wwi[...])
    u_o[...] = _dot(xb, wu[...])
    ga_o[...] = _dot(xb, wga[...])
    gb_o[...] = _dot(xb, wgb[...])


def _in_proj(x, ws):
    n, d = x.shape
    tm = _row_tile(n, TOKEN_TILE)
    widths = [w.shape[1] for w in ws]
    aw, kvw, _, qiw, kiw, wiw, uw, gw, _ = widths
    row = lambda w: pl.BlockSpec((tm, w), lambda i: (i, 0))
    out_shape = [
        jax.ShapeDtypeStruct((N_HEADS, n, HEAD_DIM), BF16),
        jax.ShapeDtypeStruct((n, kvw), F32),
        jax.ShapeDtypeStruct((n, kvw), F32),
        jax.ShapeDtypeStruct((n, kvw), BF16),
        jax.ShapeDtypeStruct((N_KV_HEADS, n, HEAD_DIM), BF16),
        jax.ShapeDtypeStruct((N_IDX_HEADS, n, IDX_DIM), BF16),
        jax.ShapeDtypeStruct((n, kiw), F32),
        jax.ShapeDtypeStruct((n, kiw), BF16),
        jax.ShapeDtypeStruct((n, wiw), F32),
        jax.ShapeDtypeStruct((n, uw), F32),
        jax.ShapeDtypeStruct((n, gw), F32),
        jax.ShapeDtypeStruct((n, gw), F32),
    ]
    out_specs = [pl.BlockSpec((N_HEADS, tm, HEAD_DIM), lambda i: (0, i, 0)),
                 row(kvw), row(kvw), row(kvw),
                 pl.BlockSpec((N_KV_HEADS, tm, HEAD_DIM), lambda i: (0, i, 0)),
                 pl.BlockSpec((N_IDX_HEADS, tm, IDX_DIM), lambda i: (0, i, 0)),
                 row(kiw), row(kiw), row(wiw), row(uw), row(gw), row(gw)]
    return pl.pallas_call(
        _in_proj_kernel,
        grid=(n // tm,),
        in_specs=[row(d)] + [_resident(w.shape) for w in ws],
        out_specs=out_specs,
        out_shape=out_shape,
        compiler_params=_cparams("parallel"),
        name="in_proj",
    )(x, *ws)


def _score_to_key(score):
    score = jnp.where(score == 0.0, 0.0, score)
    bits = lax.bitcast_convert_type(score, I32)
    return bits ^ ((bits >> 31) & 0x7FFFFFFF)


def _select_topk(keys_ref, rows, nkc, topk, idx_bits):
    ck = KEY_CHUNK
    rb = min(rows, COUNT_ROW_BLOCK)
    assert rows % rb == 0
    row_blocks = [slice(r0, r0 + rb) for r0 in range(0, rows, rb)]
    lane = lax.broadcasted_iota(I32, (rb, ck), 1)

    def count(pred):
        parts = []
        for rs in row_blocks:
            def body(c, acc, rs=rs):
                off = pl.multiple_of(c * ck, ck)
                m = jnp.where(pred(keys_ref[rs, pl.ds(off, ck)], off, rs), 1.0, 0.0)
                part = m[:, 0:LANES]
                for t in range(1, ck // LANES):
                    part = part + m[:, t * LANES:(t + 1) * LANES]
                return acc + part
            parts.append(lax.fori_loop(0, nkc, body, jnp.zeros((rb, LANES), F32)))
        acc = parts[0] if len(parts) == 1 else jnp.concatenate(parts, axis=0)
        return jnp.sum(acc, axis=1, keepdims=True)

    kf = float(topk)
    t0 = jnp.where(count(lambda kc, off, rs: kc >= 0) >= kf, 0, INT_MIN).astype(I32)

    def bit_body(j, t):
        cand = t + jnp.left_shift(jnp.int32(1), 30 - j)
        return jnp.where(count(lambda kc, off, rs: kc >= cand[rs]) >= kf, cand, t)

    t = lax.fori_loop(0, 31, bit_body, t0)
    t = jnp.maximum(t, KEY_NEG_INF + 1)

    cnt_gt = count(lambda kc, off, rs: kc > t[rs])
    cnt_ge = count(lambda kc, off, rs: kc >= t[rs])
    need = kf - cnt_gt
    tie = (cnt_ge - cnt_gt) > need

    @pl.when(jnp.max(jnp.where(tie, 1.0, 0.0)) > 0.0)
    def _():
        def idx_body(j, m):
            cand = m + jnp.left_shift(jnp.int32(1), idx_bits - 1 - j)
            c = count(lambda kc, off, rs: jnp.where(kc == t[rs], off + lane, cand[rs]) < cand[rs])
            return jnp.where(c < need, cand, m)

        m = lax.fori_loop(0, idx_bits, idx_body, jnp.zeros((rows, 1), I32))
        last = jnp.where(tie, m, 2 ** 30)

        for rs in row_blocks:
            def fix(c, carry, rs=rs):
                off = pl.multiple_of(c * ck, ck)
                kc = keys_ref[rs, pl.ds(off, ck)]
                lose = jnp.where(kc == t[rs], off + lane, -1) > last[rs]
                keys_ref[rs, pl.ds(off, ck)] = jnp.where(lose, kc - 1, kc)
                return carry

            lax.fori_loop(0, nkc, fix, 0)

    return t


def _prompt_attn_kernel(q_ref, qi_ref, wi_ref, kt_ref, v_ref, kit_ref, o_ref,
                        keys_ref, m_ref, l_ref, acc_ref, alpha_ref, s_ref, p_ref, *, topk, idx_bits):
    tq, ck = Q_TILE, KEY_CHUNK
    i = pl.program_id(1)
    row0 = i * tq
    nkc = (row0 + tq + ck - 1) // ck
    sb = min(tq, SCORE_ROW_BLOCK)
    row = row0 + lax.broadcasted_iota(I32, (sb, ck), 0)
    lane = lax.broadcasted_iota(I32, (sb, ck), 1)

    def score_body(c, carry):
        off = pl.multiple_of(c * ck, ck)
        kic = kit_ref[:, pl.ds(off, ck)]
        for r0 in range(0, tq, sb):
            rs = slice(r0, r0 + sb)
            wi = wi_ref[rs, :]
            acc = jnp.zeros((sb, ck), F32)
            for h in range(N_IDX_HEADS):
                d = _dot(qi_ref[h, rs, :], kic)
                acc = acc + wi[:, h:h + 1] * jnp.maximum(d, 0.0)
            acc = jnp.where(off + lane <= row + r0, acc, -jnp.inf)
            keys_ref[rs, pl.ds(off, ck)] = _score_to_key(acc)
        return carry

    lax.fori_loop(0, nkc, score_body, 0)
    t = _select_topk(keys_ref, tq, nkc, topk, idx_bits)

    c2 = (HEAD_DIM ** -0.5) * math.log2(math.e)
    rb = ATTN_ROW_BLOCK

    for r0 in range(0, tq, rb):
        rs = slice(r0, r0 + rb)
        n_units = ((row0 + r0 + rb + ck - 1) // ck) * N_KV_HEADS
        t_b = t[rs]
        m_ref[...] = jnp.full(m_ref.shape, NEG_BIG, F32)
        l_ref[...] = jnp.zeros(l_ref.shape, F32)
        acc_ref[...] = jnp.zeros(acc_ref.shape, F32)
        s_ref[...] = jnp.full(s_ref.shape, NEG_BIG, F32)
        p_ref[...] = jnp.zeros(p_ref.shape, BF16)
        alpha_ref[...] = jnp.ones(alpha_ref.shape, F32)

        def attn_step(n, carry, rs=rs, n_units=n_units, t_b=t_b):
            off_c = pl.multiple_of((jnp.maximum(n - 2, 0) // N_KV_HEADS) * ck, ck)
            g_c = (n + 2) % N_KV_HEADS
            slot_c = n % 2
            vc = v_ref[g_c, pl.ds(off_c, ck), :]
            for j in range(KV_GROUP):
                h = g_c * KV_GROUP + j
                acc_ref[h] = alpha_ref[slot_c, j] * acc_ref[h] + _dot(p_ref[slot_c, j], vc)
            g_b = (n + 3) % N_KV_HEADS
            slot_b = (n + 1) % 2
            for j in range(KV_GROUP):
                h = g_b * KV_GROUP + j
                s = s_ref[slot_b, j]
                m_old = m_ref[h]
                m_new = jnp.maximum(m_old, jnp.max(s, axis=1, keepdims=True))
                p = jnp.exp2((s - m_new) * c2)
                alpha = jnp.exp2((m_old - m_new) * c2)
                l_ref[h] = alpha * l_ref[h] + jnp.sum(p, axis=1, keepdims=True)
                p_ref[slot_b, j] = p.astype(BF16)
                alpha_ref[slot_b, j] = alpha
                m_ref[h] = m_new
            unit_a = jnp.minimum(n, n_units - 1)
            off_a = pl.multiple_of((unit_a // N_KV_HEADS) * ck, ck)
            g_a = unit_a % N_KV_HEADS
            t_n = jnp.where(n < n_units, t_b, 2 ** 31 - 1)
            bias = jnp.where(keys_ref[rs, pl.ds(off_a, ck)] >= t_n, 0.0, NEG_BIG)
            kc = kt_ref[g_a, :, pl.ds(off_a, ck)]
            for j in range(KV_GROUP):
                s_ref[slot_c, j] = _dot(q_ref[g_a * KV_GROUP + j, rs, :], kc) + bias
            return carry

        lax.fori_loop(0, n_units + 2, attn_step, 0)
        for h in range(N_HEADS):
            o_ref[rs, h * HEAD_DIM:(h + 1) * HEAD_DIM] = (acc_ref[h] / l_ref[h]).astype(o_ref.dtype)


def _prompt_attention(q, qi, wi, kb, vb, kib, batch, seq):
    topk = min(TOPK_MAX, seq // 4)
    assert seq % KEY_CHUNK == 0 and seq % Q_TILE == 0
    idx_bits = max(1, (seq - 1).bit_length())
    q4 = q.reshape(N_HEADS, batch, seq, HEAD_DIM)
    qi4 = qi.reshape(N_IDX_HEADS, batch, seq, IDX_DIM)
    wi3 = wi.reshape(batch, seq, N_IDX_HEADS)
    kt4 = jnp.swapaxes(kb.reshape(batch, seq, KV_WIDTH), 1, 2).reshape(batch, N_KV_HEADS, HEAD_DIM, seq)
    v4 = vb.reshape(N_KV_HEADS, batch, seq, HEAD_DIM)
    kit3 = jnp.swapaxes(kib.reshape(batch, seq, IDX_DIM), 1, 2)
    tq, rb = Q_TILE, ATTN_ROW_BLOCK
    assert tq % rb == 0
    out = pl.pallas_call(
        functools.partial(_prompt_attn_kernel, topk=topk, idx_bits=idx_bits),
        grid=(batch, seq // tq),
        in_specs=[
            pl.BlockSpec((N_HEADS, None, tq, HEAD_DIM), lambda b, i: (0, b, i, 0)),
            pl.BlockSpec((N_IDX_HEADS, None, tq, IDX_DIM), lambda b, i: (0, b, i, 0)),
            pl.BlockSpec((None, tq, N_IDX_HEADS), lambda b, i: (b, i, 0)),
            pl.BlockSpec((None, N_KV_HEADS, HEAD_DIM, seq), lambda b, i: (b, 0, 0, 0)),
            pl.BlockSpec((N_KV_HEADS, None, seq, HEAD_DIM), lambda b, i: (0, b, 0, 0)),
            pl.BlockSpec((None, IDX_DIM, seq), lambda b, i: (b, 0, 0)),
        ],
        out_specs=pl.BlockSpec((None, tq, ATTN_WIDTH), lambda b, i: (b, i, 0)),
        out_shape=jax.ShapeDtypeStruct((batch, seq, ATTN_WIDTH), BF16),
        scratch_shapes=[pltpu.VMEM((tq, seq), I32),
                        pltpu.VMEM((N_HEADS, rb, 1), F32),
                        pltpu.VMEM((N_HEADS, rb, 1), F32),
                        pltpu.VMEM((N_HEADS, rb, HEAD_DIM), F32),
                        pltpu.VMEM((2, KV_GROUP, rb, 1), F32),
                        pltpu.VMEM((2, KV_GROUP, rb, KEY_CHUNK), F32),
                        pltpu.VMEM((2, KV_GROUP, rb, KEY_CHUNK), BF16)],
        compiler_params=_cparams("parallel", "arbitrary"),
        name="prompt_attention",
    )(q4, qi4, wi3, kt4, v4, kit3)
    return out.reshape(batch * seq, ATTN_WIDTH)


def _sample_scores_kernel(pt_ref, qi_ref, wi_ref, kin_ref, *rest, n_groups):
    page_refs, keys_ref = rest[:PAGES_PER_STEP], rest[PAGES_PER_STEP]
    j = pl.program_id(1)
    qi = qi_ref[...]
    wi = wi_ref[...]

    @pl.when(j < n_groups)
    def _():
        for p in range(PAGES_PER_STEP):
            d = _dot(qi, page_refs[p][...].astype(BF16))
            s = jnp.sum(wi * jnp.maximum(d, 0.0), axis=0, keepdims=True)
            keys_ref[p:p + 1, :] = _score_to_key(s)

    @pl.when(j == n_groups)
    def _():
        kin = kin_ref[...].astype(BF16).astype(F32)
        d = jnp.sum(qi.astype(F32) * kin, axis=1, keepdims=True)
        s = jnp.sum(wi * jnp.maximum(d, 0.0), axis=0, keepdims=True)
        first = (lax.broadcasted_iota(I32, keys_ref.shape, 0) == 0) & (
            lax.broadcasted_iota(I32, keys_ref.shape, 1) == 0)
        keys_ref[...] = jnp.where(first, _score_to_key(s), KEY_NEG_INF)


def _sample_select_kernel(keys_in_ref, tri_ref, digits_ref, pos_ref, cnt_ref, keys_ref, rank_ref,
                          *, topk, idx_bits, past):
    rows, width = keys_in_ref.shape
    ck = KEY_CHUNK
    keys_ref[...] = keys_in_ref[...]
    t = _select_topk(keys_ref, rows, width // ck, topk, idx_bits)

    rank_ref[...] = jnp.full(rank_ref.shape, -1.0, F32)
    carry = jnp.zeros((rows, 1), F32)
    for c in range(past // LANES):
        sl = slice(c * LANES, (c + 1) * LANES)
        m = jnp.where(keys_ref[:, sl] >= t, 1.0, 0.0)
        incl = _dot(m.astype(BF16), tri_ref[...])
        rank_ref[:, sl] = jnp.where(m > 0.0, carry + incl - 1.0, -1.0)
        carry = carry + incl[:, LANES - 1:LANES]
    cnt_ref[...] = carry

    want = lax.broadcasted_iota(I32, (topk, 1), 0).astype(F32)

    def row_body(b, carry_):
        acc = jnp.zeros((topk, LANES), F32)
        for c in range(past // ck):
            r = rank_ref[pl.ds(b, 1), c * ck:(c + 1) * ck]
            onehot = jnp.where(r == want, 1.0, 0.0).astype(BF16)
            acc = acc + _dot(onehot, digits_ref[c * ck:(c + 1) * ck, :])
        pos_ref[b] = acc
        return carry_

    lax.fori_loop(0, rows, row_body, 0)


def _sample_attend_kernel(pos_ref, cnt_ref, pt_ref, q_ref, kn_ref, vn_ref, ck_hbm, cv_hbm, o_ref,
                          kbuf, vbuf, sem, *, layer, n_pages, page, topk):
    b = pl.program_id(0)
    kvh = N_KV_HEADS

    def row_copies(bb, i, slot):
        pos = pos_ref[bb * topk + i]
        pg = pt_ref[bb * n_pages + pos // page]
        src = pl.ds(pl.multiple_of((pos % page) * kvh, kvh), kvh)
        dst = pl.ds(pl.multiple_of(i * kvh, kvh), kvh)
        return (pltpu.make_async_copy(ck_hbm.at[layer, pg, src, :], kbuf.at[slot, dst, :], sem.at[0, slot]),
                pltpu.make_async_copy(cv_hbm.at[layer, pg, src, :], vbuf.at[slot, dst, :], sem.at[1, slot]))

    def start_all(bb, slot):
        def body(i, carry):
            for cp in row_copies(bb, i, slot):
                cp.start()
            return carry
        lax.fori_loop(0, topk, body, 0)

    @pl.when(b == 0)
    def _():
        start_all(0, 0)

    @pl.when(b + 1 < pl.num_programs(0))
    def _():
        start_all(b + 1, (b + 1) % 2)

    slot = b % 2

    def wait_body(i, carry):
        for cp in row_copies(b, i, slot):
            cp.wait()
        return carry

    lax.fori_loop(0, topk, wait_body, 0)

    scale = HEAD_DIM ** -0.5
    q = q_ref[...]
    n_rows = topk * kvh
    col = lax.broadcasted_iota(I32, (N_HEADS, n_rows), 1)
    head_group = lax.broadcasted_iota(I32, (N_HEADS, n_rows), 0) // KV_GROUP
    cnt = cnt_ref[b]
    valid = (col % kvh == head_group) & (col // kvh < cnt)
    s = jnp.where(valid, _dot_nt(q, kbuf[slot].astype(BF16)) * scale, NEG_BIG)

    kn = kn_ref[...].astype(BF16).astype(F32)
    vn = vn_ref[...].astype(BF16).astype(F32)
    qf = q.astype(F32)
    row_group = lax.broadcasted_iota(I32, (N_HEADS, 1), 0) // KV_GROUP
    s_new = jnp.zeros((N_HEADS, 1), F32)
    v_new = jnp.zeros((N_HEADS, HEAD_DIM), F32)
    for g in range(kvh):
        sl = slice(g * HEAD_DIM, (g + 1) * HEAD_DIM)
        s_new = jnp.where(row_group == g, jnp.sum(qf * kn[:, sl], axis=1, keepdims=True), s_new)
        v_new = jnp.where(row_group == g, vn[:, sl], v_new)
    new_sel = cnt < topk
    s_new = jnp.where(new_sel, s_new * scale, NEG_BIG)

    m = jnp.maximum(jnp.max(s, axis=1, keepdims=True), s_new)
    p = jnp.where(valid, jnp.exp(s - m), 0.0)
    p_new = jnp.where(new_sel, jnp.exp(s_new - m), 0.0)
    l = jnp.sum(p, axis=1, keepdims=True) + p_new
    acc = _dot(p.astype(BF16), vbuf[slot].astype(BF16)) + p_new.astype(BF16).astype(F32) * v_new
    o_ref[...] = (acc / l).astype(o_ref.dtype)


def _sample_attention(q, qi, wi, k_new, v_new, ki_new, cache_k, cache_v, cache_kidx, page_table, layer):
    db, n_pages = page_table.shape
    page = cache_k.shape[2]
    npg = PAGES_PER_STEP
    assert n_pages % npg == 0 and page == LANES
    n_groups = n_pages // npg
    past = n_pages * page
    topk = min(TOPK_MAX, (past + 1) // 4)
    n_rows = n_pages + npg
    width = n_rows * page
    assert width % KEY_CHUNK == 0
    idx_bits = (width - 1).bit_length()
    pt = page_table.reshape(-1).astype(I32)

    page_rows = page * N_KV_HEADS
    ck4 = cache_k.reshape(cache_k.shape[0], cache_k.shape[1], page_rows, HEAD_DIM)
    cv4 = cache_v.reshape(cache_v.shape[0], cache_v.shape[1], page_rows, HEAD_DIM)
    kidx_t = jnp.swapaxes(cache_kidx, 2, 3)
    qi3 = jnp.transpose(qi, (1, 0, 2))
    wi3 = wi.reshape(db, N_IDX_HEADS, 1)
    q3 = jnp.transpose(q, (1, 0, 2))

    def page_spec(rows_, width_, p):
        def imap(b, j, pt_ref):
            pg = jnp.minimum(j * npg + p, n_pages - 1)
            return (layer, pt_ref[b * n_pages + pg], 0, 0)
        return pl.BlockSpec((None, None, rows_, width_), imap)

    keys = pl.pallas_call(
        functools.partial(_sample_scores_kernel, n_groups=n_groups),
        grid_spec=pltpu.PrefetchScalarGridSpec(
            num_scalar_prefetch=1,
            grid=(db, n_groups + 1),
            in_specs=[
                pl.BlockSpec((None, N_IDX_HEADS, IDX_DIM), lambda b, j, pt_ref: (b, 0, 0)),
                pl.BlockSpec((None, N_IDX_HEADS, 1), lambda b, j, pt_ref: (b, 0, 0)),
                pl.BlockSpec((None, 1, IDX_DIM), lambda b, j, pt_ref: (b, 0, 0)),
            ] + [page_spec(IDX_DIM, page, p) for p in range(npg)],
            out_specs=pl.BlockSpec((None, npg, page), lambda b, j, pt_ref: (b, j, 0)),
        ),
        out_shape=jax.ShapeDtypeStruct((db, n_rows, page), I32),
        compiler_params=_cparams("parallel", "arbitrary"),
        name="sample_scores",
    )(pt, qi3, wi3, ki_new.reshape(db, 1, IDX_DIM), *([kidx_t] * npg))

    tri = (jnp.arange(LANES)[:, None] <= jnp.arange(LANES)[None, :]).astype(BF16)
    positions = jnp.arange(past)
    digits = jnp.zeros((past, LANES), BF16).at[:, 0].set((positions // LANES).astype(BF16))
    digits = digits.at[:, 1].set((positions % LANES).astype(BF16))
    pos_digits, cnt = pl.pallas_call(
        functools.partial(_sample_select_kernel, topk=topk, idx_bits=idx_bits, past=past),
        out_shape=[jax.ShapeDtypeStruct((db, topk, LANES), F32), jax.ShapeDtypeStruct((db, 1), F32)],
        scratch_shapes=[pltpu.VMEM((db, width), I32), pltpu.VMEM((db, width), F32)],
        compiler_params=pltpu.CompilerParams(vmem_limit_bytes=VMEM_LIMIT_BYTES),
        name="sample_select",
    )(keys.reshape(db, width), tri, digits)
    pos = (pos_digits[:, :, 0] * LANES + pos_digits[:, :, 1]).astype(I32).reshape(-1)
    cnt = cnt.astype(I32).reshape(-1)

    out = pl.pallas_call(
        functools.partial(_sample_attend_kernel, layer=layer, n_pages=n_pages, page=page, topk=topk),
        grid_spec=pltpu.PrefetchScalarGridSpec(
            num_scalar_prefetch=3,
            grid=(db,),
            in_specs=[
                pl.BlockSpec((None, N_HEADS, HEAD_DIM), lambda b, *_: (b, 0, 0)),
                pl.BlockSpec((None, 1, KV_WIDTH), lambda b, *_: (b, 0, 0)),
                pl.BlockSpec((None, 1, KV_WIDTH), lambda b, *_: (b, 0, 0)),
                pl.BlockSpec(memory_space=pl.ANY),
                pl.BlockSpec(memory_space=pl.ANY),
            ],
            out_specs=pl.BlockSpec((None, N_HEADS, HEAD_DIM), lambda b, *_: (b, 0, 0)),
            scratch_shapes=[pltpu.VMEM((2, topk * N_KV_HEADS, HEAD_DIM), F32),
                            pltpu.VMEM((2, topk * N_KV_HEADS, HEAD_DIM), F32),
                            pltpu.SemaphoreType.DMA((2, 2))],
        ),
        out_shape=jax.ShapeDtypeStruct((db, N_HEADS, HEAD_DIM), BF16),
        compiler_params=_cparams("arbitrary"),
        name="sample_attend",
    )(pos, cnt, pt, q3, k_new.reshape(db, 1, KV_WIDTH), v_new.reshape(db, 1, KV_WIDTH), ck4, cv4)
    return out.reshape(db, ATTN_WIDTH)


def _s5_scan_kernel(u_ref, h0_ref, a_ref, bbd_ref, cbd_ref, d_ref, wglu_ref, bglu_ref,
                    *out_refs, n_steps, emit_y):
    if emit_y:
        y_ref, hend_ref = out_refs
    else:
        (hend_ref,) = out_refs
    width = SSM_GROUPS_PER_BLOCK * SSM_GROUP * (bbd_ref.shape[0])
    nb = bbd_ref.shape[0]
    bs = SSM_BLOCK_STATES
    h = [None] * nb
    for blk in range(nb):
        h[blk] = (h0_ref[:, blk * 2 * bs:blk * 2 * bs + bs], h0_ref[:, blk * 2 * bs + bs:(blk + 1) * 2 * bs])
    for tau in range(n_steps):
        u = u_ref[:, tau * width:(tau + 1) * width]
        ub = u.astype(BF16)
        ys = []
        for blk in range(nb):
            bu = _dot(ub[:, blk * LANES:(blk + 1) * LANES], bbd_ref[blk])
            ar = a_ref[0:1, blk * 2 * bs:blk * 2 * bs + bs]
            ai = a_ref[0:1, blk * 2 * bs + bs:(blk + 1) * 2 * bs]
            hr, hi = h[blk]
            nr = ar * hr - ai * hi + bu[:, :bs]
            ni = ar * hi + ai * hr + bu[:, bs:]
            h[blk] = (nr, ni)
            if emit_y:
                ys.append(_dot(nr.astype(BF16), cbd_ref[blk, :bs, :])
                          - _dot(ni.astype(BF16), cbd_ref[blk, bs:, :]))
        if emit_y:
            y = jnp.concatenate(ys, axis=1) + d_ref[...] * u
            y = jax.nn.gelu(y)
            z = _dot(y.astype(BF16), wglu_ref[...]) + bglu_ref[...]
            y_ref[:, tau * width:(tau + 1) * width] = (y * jax.nn.sigmoid(z)).astype(y_ref.dtype)
    for blk in range(nb):
        hend_ref[:, blk * 2 * bs:blk * 2 * bs + bs] = h[blk][0]
        hend_ref[:, blk * 2 * bs + bs:(blk + 1) * 2 * bs] = h[blk][1]


def _s5_scan(u_rows, h0, p, n_steps, emit_y):
    r = u_rows.shape[0]
    tr = _row_tile(r, SSM_ROW_TILE)
    hw = h0.shape[1]
    row = lambda w: pl.BlockSpec((tr, w), lambda i: (i, 0))
    out_shape = [jax.ShapeDtypeStruct((r, hw), F32)]
    out_specs = [row(hw)]
    if emit_y:
        out_shape = [jax.ShapeDtypeStruct(u_rows.shape, BF16)] + out_shape
        out_specs = [row(u_rows.shape[1])] + out_specs
    consts = [p["a"], p["bbd"], p["cbd"], p["d"], p["wglu"], p["bglu"]]
    return pl.pallas_call(
        functools.partial(_s5_scan_kernel, n_steps=n_steps, emit_y=emit_y),
        grid=(r // tr,),
        in_specs=[row(u_rows.shape[1]), row(hw)] + [_resident(c.shape) for c in consts],
        out_specs=out_specs,
        out_shape=out_shape,
        compiler_params=_cparams("parallel"),
        name="s5_scan_y" if emit_y else "s5_scan_state",
    )(u_rows, h0, *consts)


def _s5_carry_kernel(s_ref, a_ref, hprev_ref, hend_ref, *, n_chunks, chunk_len):
    hw = s_ref.shape[1]
    piece = SSM_BLOCK_STATES
    for blk in range(hw // (2 * piece)):
        re = pl.ds(blk * 2 * piece, piece)
        im = pl.ds(blk * 2 * piece + piece, piece)
        ar, ai = a_ref[0:1, re], a_ref[0:1, im]
        pr, pi = ar, ai
        for _ in range(chunk_len - 1):
            pr, pi = pr * ar - pi * ai, pr * ai + pi * ar

        def body(c, carry, re=re, im=im, pr=pr, pi=pi):
            hr, hi = carry
            hprev_ref[pl.ds(c, 1), re] = hr
            hprev_ref[pl.ds(c, 1), im] = hi
            sr = s_ref[pl.ds(c, 1), re]
            si = s_ref[pl.ds(c, 1), im]
            return pr * hr - pi * hi + sr, pr * hi + pi * hr + si

        zero = jnp.zeros((1, piece), F32)
        hr, hi = lax.fori_loop(0, n_chunks, body, (zero, zero))
        hend_ref[0:1, re] = hr
        hend_ref[0:1, im] = hi


def _s5_carry(s, a, batch, n_chunks, chunk_len):
    hw = s.shape[1]
    s3 = s.reshape(batch, n_chunks, hw)
    hprev, hend = pl.pallas_call(
        functools.partial(_s5_carry_kernel, n_chunks=n_chunks, chunk_len=chunk_len),
        grid=(batch,),
        in_specs=[pl.BlockSpec((None, n_chunks, hw), lambda b: (b, 0, 0)), _resident(a.shape)],
        out_specs=[pl.BlockSpec((None, n_chunks, hw), lambda b: (b, 0, 0)),
                   pl.BlockSpec((None, 1, hw), lambda b: (b, 0, 0))],
        out_shape=[jax.ShapeDtypeStruct((batch, n_chunks, hw), F32),
                   jax.ShapeDtypeStruct((batch, 1, hw), F32)],
        compiler_params=_cparams("parallel"),
        name="s5_carry",
    )(s3, a)
    return hprev.reshape(batch * n_chunks, hw), hend.reshape(batch, hw)


def _s5_params(a_re, a_im, log_dt, b_re, b_im, c_re, c_im, d_skip, w_glu, b_glu):
    g, p = a_re.shape
    c = b_re.shape[2]
    gpb = SSM_GROUPS_PER_BLOCK
    nb = g // gpb
    lr, li = a_re.astype(F32), a_im.astype(F32)
    dt = jnp.exp(log_dt.astype(F32))[:, None]
    mag = jnp.exp(lr * dt)
    ab_re, ab_im = mag * jnp.cos(li * dt), mag * jnp.sin(li * dt)
    den = lr * lr + li * li
    nr = ab_re - 1.0
    f_re = (nr * lr + ab_im * li) / den
    f_im = (ab_im * lr - nr * li) / den
    br, bi = b_re.astype(F32), b_im.astype(F32)
    bb_re = f_re[..., None] * br - f_im[..., None] * bi
    bb_im = f_re[..., None] * bi + f_im[..., None] * br
    eye = jnp.eye(gpb, dtype=F32)

    def state_layout(re, im):
        return jnp.concatenate([re.reshape(nb, gpb * p), im.reshape(nb, gpb * p)], axis=1).reshape(1, -1)

    def in_block(x):
        x = jnp.transpose(x.reshape(nb, gpb, p, c), (0, 1, 3, 2))
        return (x[:, :, :, None, :] * eye[None, :, None, :, None]).reshape(nb, gpb * c, gpb * p)

    def out_block(x):
        x = jnp.transpose(x.reshape(nb, gpb, c, p), (0, 1, 3, 2))
        return (x[:, :, :, None, :] * eye[None, :, None, :, None]).reshape(nb, gpb * p, gpb * c)

    bbd = jnp.concatenate([in_block(bb_re), in_block(bb_im)], axis=2).astype(BF16)
    cbd = jnp.concatenate([out_block(c_re.astype(F32)), out_block(c_im.astype(F32))], axis=1).astype(BF16)
    return {"a": state_layout(ab_re, ab_im), "bbd": bbd, "cbd": cbd,
            "d": d_skip.astype(F32).reshape(1, -1), "wglu": w_glu.astype(BF16),
            "bglu": b_glu.astype(F32).reshape(1, -1)}


def _to_state_layout(re, im):
    b, g, p = re.shape
    nb = g // SSM_GROUPS_PER_BLOCK
    return jnp.concatenate([re.reshape(b, nb, -1), im.reshape(b, nb, -1)], axis=2).reshape(b, -1)


def _from_state_layout(h, g, p):
    b = h.shape[0]
    nb = g // SSM_GROUPS_PER_BLOCK
    h4 = h.reshape(b, nb, 2, SSM_GROUPS_PER_BLOCK * p)
    return h4[:, :, 0].reshape(b, g, p), h4[:, :, 1].reshape(b, g, p)


def _s5_prompt(u, p, batch, seq, g, n_state):
    w = u.shape[1]
    lc = SSM_CHUNK
    assert seq % lc == 0
    n_chunks = seq // lc
    rows = batch * n_chunks
    u_rows = u.reshape(rows, lc * w)
    zeros = jnp.zeros((rows, p["a"].shape[1]), F32)
    (s_loc,) = _s5_scan(u_rows, zeros, p, lc, emit_y=False)
    hprev, hend = _s5_carry(s_loc, p["a"], batch, n_chunks, lc)
    y, _ = _s5_scan(u_rows, hprev, p, lc, emit_y=True)
    hr, hi = _from_state_layout(hend, g, n_state)
    return y.reshape(batch * seq, w), hr, hi


def _s5_sample(u, h0_re, h0_im, p):
    g, n_state = h0_re.shape[1:]
    y, hend = _s5_scan(u, _to_state_layout(h0_re.astype(F32), h0_im.astype(F32)), p, 1, emit_y=True)
    hr, hi = _from_state_layout(hend, g, n_state)
    return y, hr, hi


def _mix_kernel(x_ref, attn_ref, ssm_ref, ga_ref, gb_ref, wa_ref, ws_ref, wo_ref, g_ref, b_ref, o_ref, *, alpha):
    a = _dot(attn_ref[...], wa_ref[...])
    s = _dot(ssm_ref[...], ws_ref[...])
    merged = jax.nn.sigmoid(ga_ref[...]) * a + jax.nn.sigmoid(gb_ref[...]) * s
    mix = _dot(merged.astype(BF16), wo_ref[...])
    o_ref[...] = _layer_norm(alpha * x_ref[...] + mix, g_ref[...], b_ref[...])


def _mix(x, attn, ssm, ga, gb, wa, ws, wo, g, b, alpha):
    n, d = x.shape
    tm = _row_tile(n, TOKEN_TILE)
    row = lambda w: pl.BlockSpec((tm, w), lambda i: (i, 0))
    return pl.pallas_call(
        functools.partial(_mix_kernel, alpha=alpha),
        grid=(n // tm,),
        in_specs=[row(d), row(attn.shape[1]), row(ssm.shape[1]), row(d), row(d),
                  _resident(wa.shape), _resident(ws.shape), _resident(wo.shape),
                  _resident(g.shape), _resident(b.shape)],
        out_specs=row(d),
        out_shape=jax.ShapeDtypeStruct((n, d), F32),
        compiler_params=_cparams("parallel"),
        name="mix_out_ln",
    )(x, attn, ssm, ga, gb, wa, ws, wo, g, b)


def _ffn_kernel(x_ref, w1_ref, w3_ref, w2_ref, g_ref, b_ref, o_ref, *, alpha, f_chunk):
    x = x_ref[...]
    xb = x.astype(BF16)
    acc = jnp.zeros(x.shape, F32)
    for c in range(w1_ref.shape[1] // f_chunk):
        sl = slice(c * f_chunk, (c + 1) * f_chunk)
        h = jax.nn.silu(_dot(xb, w1_ref[:, sl])) * _dot(xb, w3_ref[:, sl])
        acc = acc + _dot(h.astype(BF16), w2_ref[sl, :])
    o_ref[...] = _layer_norm(alpha * x + acc, g_ref[...], b_ref[...])


def _ffn_chunk(d_ff):
    best = LANES
    for c in range(LANES, d_ff + 1, LANES):
        if d_ff % c == 0 and c <= 1536:
            best = c
    return best


def _ffn(x, w1, w3, w2, g, b, alpha):
    n, d = x.shape
    tm = _row_tile(n, TOKEN_TILE)
    row = pl.BlockSpec((tm, d), lambda i: (i, 0))
    return pl.pallas_call(
        functools.partial(_ffn_kernel, alpha=alpha, f_chunk=_ffn_chunk(w1.shape[1])),
        grid=(n // tm,),
        in_specs=[row, _resident(w1.shape), _resident(w3.shape), _resident(w2.shape),
                  _resident(g.shape), _resident(b.shape)],
        out_specs=row,
        out_shape=jax.ShapeDtypeStruct((n, d), F32),
        compiler_params=_cparams("parallel"),
        name="ffn_ln",
    )(x, w1, w3, w2, g, b)


def _moe_kernel(x_ref, wr_ref, br_ref, w1_ref, w3_ref, w2_ref, g_ref, b_ref, o_ref, comb_ref, acc_ref, *, alpha):
    e = pl.program_id(1)
    x = x_ref[...]
    xb = x.astype(BF16)
    n_exp = comb_ref.shape[1]
    lane = lax.broadcasted_iota(I32, comb_ref.shape, 1).astype(F32)

    @pl.when(e == 0)
    def _():
        logits = _dot(xb, wr_ref[...]) + br_ref[...]
        m1 = jnp.max(logits, axis=1, keepdims=True)
        i1 = jnp.min(jnp.where(logits == m1, lane, float(n_exp)), axis=1, keepdims=True)
        rest = jnp.where(lane == i1, -jnp.inf, logits)
        m2 = jnp.max(rest, axis=1, keepdims=True)
        i2 = jnp.min(jnp.where(rest == m2, lane, float(n_exp)), axis=1, keepdims=True)
        e2 = jnp.exp(m2 - m1)
        den = 1.0 + e2
        comb_ref[...] = jnp.where(lane == i1, 1.0 / den, 0.0) + jnp.where(lane == i2, e2 / den, 0.0)
        acc_ref[...] = jnp.zeros(acc_ref.shape, F32)

    h = jax.nn.silu(_dot(xb, w1_ref[...])) * _dot(xb, w3_ref[...])
    y = _dot(h.astype(BF16), w2_ref[...])
    ce = jnp.sum(jnp.where(lane == e.astype(F32), comb_ref[...], 0.0), axis=1, keepdims=True)
    acc_ref[...] = acc_ref[...] + ce * y

    @pl.when(e == n_exp - 1)
    def _():
        o_ref[...] = _layer_norm(alpha * x + acc_ref[...], g_ref[...], b_ref[...])


def _moe_routed_kernel(x_ref, wr_ref, br_ref, tril_ref, triu_ref, w1_ref, w3_ref, w2_ref, g_ref, b_ref, o_ref,
                       comb_ref, rankc_ref, rankr_ref, acc_ref, xb_ref, *, alpha, rb):
    e = pl.program_id(1)
    tm = x_ref.shape[0]
    lane = lax.broadcasted_iota(I32, (tm, LANES), 1).astype(F32)

    @pl.when(e == 0)
    def _():
        xb_ref[...] = x_ref[...].astype(BF16)
        logits = _dot(xb_ref[...], wr_ref[...]) + br_ref[...]
        m1 = jnp.max(logits, axis=1, keepdims=True)
        i1 = jnp.min(jnp.where(logits == m1, lane, float(LANES)), axis=1, keepdims=True)
        rest = jnp.where(lane == i1, -jnp.inf, logits)
        m2 = jnp.max(rest, axis=1, keepdims=True)
        i2 = jnp.min(jnp.where(rest == m2, lane, float(LANES)), axis=1, keepdims=True)
        e2 = jnp.exp(m2 - m1)
        den = 1.0 + e2
        comb = jnp.where(lane == i1, 1.0 / den, 0.0) + jnp.where(lane == i2, e2 / den, 0.0)
        comb_ref[...] = comb
        sel = jnp.where(comb > 0.0, 1.0, 0.0)
        rankc_ref[...] = jnp.where(comb > 0.0, _dot(tril_ref[...], sel.astype(BF16)), -1.0)
        sel_t = jnp.transpose(sel)
        rankr_ref[...] = jnp.where(sel_t > 0.0, _dot(sel_t.astype(BF16), triu_ref[...]), -1.0)
        acc_ref[...] = jnp.zeros(acc_ref.shape, F32)

    own = lane == e.astype(F32)
    ce = jnp.sum(jnp.where(own, comb_ref[...], 0.0), axis=1, keepdims=True)
    rank_col = jnp.sum(jnp.where(own, rankc_ref[...], 0.0), axis=1, keepdims=True)
    rank_row = rankr_ref[pl.ds(e, 1), :]
    cnt = jnp.sum(jnp.where(rank_col >= 0.0, 1.0, 0.0)).astype(I32)
    want_col = lax.broadcasted_iota(I32, (rb, 1), 0).astype(F32)
    want_row = lax.broadcasted_iota(I32, (1, rb), 1).astype(F32)

    def block(blk, carry):
        base = (blk * rb).astype(F32)
        gather = jnp.where(rank_row == want_col + base, 1.0, 0.0).astype(BF16)
        xe = _dot(gather, xb_ref[...]).astype(BF16)
        h = jax.nn.silu(_dot(xe, w1_ref[...])) * _dot(xe, w3_ref[...])
        ye = _dot(h.astype(BF16), w2_ref[...])
        hi = ye.astype(BF16)
        lo = (ye - hi.astype(F32)).astype(BF16)
        scatter = jnp.where(rank_col == want_row + base, 1.0, 0.0).astype(BF16)
        acc_ref[...] = acc_ref[...] + ce * (_dot(scatter, hi) + _dot(scatter, lo))
        return carry

    lax.fori_loop(0, (cnt + rb - 1) // rb, block, 0)

    @pl.when(e == pl.num_programs(1) - 1)
    def _():
        o_ref[...] = _layer_norm(alpha * x_ref[...] + acc_ref[...], g_ref[...], b_ref[...])


def _moe_routed(x, wr, br, w1, w3, w2, g, b, alpha):
    n, d = x.shape
    n_exp, _, f = w1.shape
    tm = MOE_TOKEN_TILE
    assert n % tm == 0 and n_exp <= LANES
    wr_pad = jnp.zeros((d, LANES), BF16).at[:, :n_exp].set(wr)
    br_pad = jnp.full((1, LANES), NEG_BIG, F32).at[:, :n_exp].set(br)
    before = jnp.arange(tm)[None, :] < jnp.arange(tm)[:, None]
    tril, triu = before.astype(BF16), before.T.astype(BF16)
    row = pl.BlockSpec((tm, d), lambda i, e: (i, 0), pipeline_mode=pl.Buffered(1))
    return pl.pallas_call(
        functools.partial(_moe_routed_kernel, alpha=alpha, rb=MOE_EXPERT_ROWS),
        grid=(n // tm, n_exp),
        in_specs=[row, _resident(wr_pad.shape), _resident(br_pad.shape),
                  _resident(tril.shape), _resident(triu.shape),
                  pl.BlockSpec((None, d, f), lambda i, e: (e, 0, 0)),
                  pl.BlockSpec((None, d, f), lambda i, e: (e, 0, 0)),
                  pl.BlockSpec((None, f, d), lambda i, e: (e, 0, 0)),
                  _resident(g.shape), _resident(b.shape)],
        out_specs=row,
        out_shape=jax.ShapeDtypeStruct((n, d), F32),
        scratch_shapes=[pltpu.VMEM((tm, LANES), F32), pltpu.VMEM((tm, LANES), F32),
                        pltpu.VMEM((LANES, tm), F32), pltpu.VMEM((tm, d), F32), pltpu.VMEM((tm, d), BF16)],
        compiler_params=_cparams("parallel", "arbitrary"),
        name="moe_routed_ln",
    )(x, wr_pad, br_pad, tril, triu, w1, w3, w2, g, b)


def _moe(x, wr, br, w1, w3, w2, g, b, alpha):
    n, d = x.shape
    n_exp, _, f = w1.shape
    tm = _row_tile(n, 2 * TOKEN_TILE)
    row = pl.BlockSpec((tm, d), lambda i, e: (i, 0))
    return pl.pallas_call(
        functools.partial(_moe_kernel, alpha=alpha),
        grid=(n // tm, n_exp),
        in_specs=[row, _resident(wr.shape), _resident(br.shape),
                  pl.BlockSpec((None, d, f), lambda i, e: (e, 0, 0)),
                  pl.BlockSpec((None, d, f), lambda i, e: (e, 0, 0)),
                  pl.BlockSpec((None, f, d), lambda i, e: (e, 0, 0)),
                  _resident(g.shape), _resident(b.shape)],
        out_specs=row,
        out_shape=jax.ShapeDtypeStruct((n, d), F32),
        scratch_shapes=[pltpu.VMEM((tm, n_exp), F32), pltpu.VMEM((tm, d), F32)],
        compiler_params=_cparams("parallel", "arbitrary"),
        name="moe_ln",
    )(x, wr, br, w1, w3, w2, g, b)


def kernel(x_prompt, x_sample, cache_k, cache_v, cache_kidx, state_ssm_re, state_ssm_im, page_table, ln1_g, ln1_b, w_in, w_attn_proj, w_ssm_proj, w_out, ssm_a_re, ssm_a_im, ssm_log_dt, ssm_b_re, ssm_b_im, ssm_c_re, ssm_c_im, ssm_d, ssm_w_glu, ssm_b_glu, ln2_g, ln2_b, ffn_w1, ffn_w3, ffn_w2, moe_w_router, moe_b_router, moe_w1, moe_w3, moe_w2):
    batch, seq, d_model = x_prompt.shape
    dec_batch, dec_seq, _ = x_sample.shape
    assert dec_seq == 1
    depth = w_in.shape[0]
    g_ssm, n_state = ssm_a_re.shape[1:]
    ssm_width = ssm_d.shape[1]
    alpha = (2.0 * depth) ** 0.25
    in_sizes = (ATTN_WIDTH, KV_WIDTH, KV_WIDTH, N_IDX_HEADS * IDX_DIM, IDX_DIM, N_IDX_HEADS,
                ssm_width, d_model, d_model)
    assert sum(in_sizes) == w_in.shape[2]
    splits = np.cumsum(np.array(in_sizes))[:-1].tolist()

    xp = x_prompt.reshape(batch * seq, d_model).astype(F32)
    xs = x_sample.reshape(dec_batch, d_model).astype(F32)
    row2 = lambda v: v.astype(F32).reshape(1, -1)

    outs = [[] for _ in range(10)]
    for l in range(depth):
        w_pieces = [w.astype(BF16) for w in jnp.split(w_in[l], splits, axis=1)]
        wa, ws, wo = w_attn_proj[l].astype(BF16), w_ssm_proj[l].astype(BF16), w_out[l].astype(BF16)
        sp = _s5_params(ssm_a_re[l], ssm_a_im[l], ssm_log_dt[l], ssm_b_re[l], ssm_b_im[l],
                        ssm_c_re[l], ssm_c_im[l], ssm_d[l], ssm_w_glu[l], ssm_b_glu[l])
        g1, b1, g2, b2 = row2(ln1_g[l]), row2(ln1_b[l]), row2(ln2_g[l]), row2(ln2_b[l])

        q, k, v, kb, vb, qi, ki, kib, wi, u, ga, gb = _in_proj(xp, w_pieces)
        attn = _prompt_attention(q, qi, wi, kb, vb, kib, batch, seq)
        ssm, hr_p, hi_p = _s5_prompt(u, sp, batch, seq, g_ssm, n_state)
        xp = _mix(xp, attn, ssm, ga, gb, wa, ws, wo, g1, b1, alpha)
        q_s, k_s, v_s, _, _, qi_s, ki_s, _, wi_s, u_s, ga_s, gb_s = _in_proj(xs, w_pieces)
        attn_s = _sample_attention(q_s, qi_s, wi_s, k_s, v_s, ki_s, cache_k, cache_v, cache_kidx,
                                   page_table, l)
        ssm_s, hr_s, hi_s = _s5_sample(u_s, state_ssm_re[l], state_ssm_im[l], sp)
        xs = _mix(xs, attn_s, ssm_s, ga_s, gb_s, wa, ws, wo, g1, b1, alpha)
        j = l // 2
        if l % 2 == 0:
            w1, w3, w2 = ffn_w1[j].astype(BF16), ffn_w3[j].astype(BF16), ffn_w2[j].astype(BF16)
            xp = _ffn(xp, w1, w3, w2, g2, b2, alpha)
            xs = _ffn(xs, w1, w3, w2, g2, b2, alpha)
        else:
            wr, br = moe_w_router[j].astype(BF16), row2(moe_b_router[j])
            w1, w3, w2 = moe_w1[j].astype(BF16), moe_w3[j].astype(BF16), moe_w2[j].astype(BF16)
            moe_p = _moe_routed if xp.shape[0] % MOE_TOKEN_TILE == 0 else _moe
            xp = moe_p(xp, wr, br, w1, w3, w2, g2, b2, alpha)
            xs = _moe(xs, wr, br, w1, w3, w2, g2, b2, alpha)

        kv_shape = (batch, seq, N_KV_HEADS, HEAD_DIM)
        kvs_shape = (dec_batch, dec_seq, N_KV_HEADS, HEAD_DIM)
        for lst, val in zip(outs, (k.reshape(kv_shape), v.reshape(kv_shape), ki.reshape(batch, seq, IDX_DIM),
                                   hr_p, hi_p, k_s.reshape(kvs_shape), v_s.reshape(kvs_shape),
                                   ki_s.reshape(dec_batch, dec_seq, IDX_DIM), hr_s, hi_s)):
            lst.append(val)

    return (xp.reshape(batch, seq, d_model), xs.reshape(dec_batch, dec_seq, d_model),
            *[jnp.stack(o) for o in outs])
```
